```python
import math
import jax, jax.numpy as jnp
from jax import lax
import numpy as np

D_MODEL = 1024
BATCH = 16
SEQ = 4096
DEPTH = 4

GRID_W = 64
CTX_LEN = 256
D_MIX = 2 * D_MODEL
GROUP_W = D_MIX // 4
A_HEAD = 128
A_HEADS = GROUP_W // A_HEAD
A_CHUNK = 32
LB_FLOOR = 1e-30
B_HEADDIM = 64
B_HEADS = GROUP_W // B_HEADDIM
B_GROUPS = 2
B_STATE = 128
B_CONV = 4
XBC_W = GROUP_W + 2 * B_GROUPS * B_STATE
C_BLOCKS = 8
C_BLOCK = GROUP_W // C_BLOCKS
C_CONV = 4
C_POW = 8.0
D_HEAD = 128
D_HEADS = GROUP_W // D_HEAD
ROPE_BASE = 10000.0
SCAN_CHUNK = 64
CONV4_PAD = (2, 1)
D_FF = 2816
FFN_CONV = 3
ALPHA = (2 * DEPTH) ** 0.25
BETA = (8 * DEPTH) ** -0.25
EPS = 1e-6
IN_SPLITS = (GROUP_W,) * 5 + (GROUP_W, XBC_W, 2 * B_HEADS) + (GROUP_W,) * 2 + (GROUP_W,) * 4
D_IN = sum(IN_SPLITS)

kernel_name = "hybrid_hgrn2_ssd_rglru_retention_dit"


def layer_norm(x, g, b):
    xf = x.astype(jnp.float32)
    mu = jnp.mean(xf, -1, keepdims=True)
    var = jnp.mean(jnp.square(xf - mu), -1, keepdims=True)
    return ((xf - mu) * lax.rsqrt(var + EPS)).astype(x.dtype) * g + b


def head_norm(x):
    xf = x.astype(jnp.float32)
    mu = jnp.mean(xf, -1, keepdims=True)
    var = jnp.mean(jnp.square(xf - mu), -1, keepdims=True)
    return ((xf - mu) * lax.rsqrt(var + EPS)).astype(x.dtype)


def rms_norm(x, w):
    xf = x.astype(jnp.float32)
    return (xf * lax.rsqrt(jnp.mean(jnp.square(xf), -1, keepdims=True) + EPS)).astype(x.dtype) * w


def seg_flip(a):
    return jnp.concatenate([jnp.flip(a[:, :CTX_LEN], 1), jnp.flip(a[:, CTX_LEN:], 1)], axis=1)


def dir_stack(fwd, bwd):
    return jnp.concatenate([fwd, seg_flip(bwd)], axis=0)


def dir_merge(y):
    nb = y.shape[0] // 2
    return y[:nb] + seg_flip(y[nb:])


def dwconv(u, w, b, pad):
    out = lax.conv_general_dilated(u, w[:, None, :], window_strides=(1,), padding=[pad],
                                   dimension_numbers=("NWC", "WIO", "NWC"), feature_group_count=u.shape[-1])
    return out + b


def seg_dwconv(u, w, b, pad):
    return jnp.concatenate([dwconv(u[:, :CTX_LEN], w, b, pad), dwconv(u[:, CTX_LEN:], w, b, pad)], axis=1)


def masked_decay(mask, diff):
    return jnp.where(mask, jnp.exp(jnp.where(mask, diff, 0.0)), 0.0)


def gla_chunk_scan(q, k, v, log_f, chunk):
    nb, L, H, K = q.shape
    V = v.shape[-1]
    nc = L // chunk
    to_chunks = lambda a: jnp.moveaxis(a.astype(jnp.float32).reshape(nb, nc, chunk, *a.shape[2:]), 1, 0)
    mask = jnp.tril(jnp.ones((chunk, chunk), bool))[None, :, :, None, None]

    def step(S, inp):
        qc, kc, vc, lc = inp
        b = jnp.cumsum(lc, axis=1)
        rel = masked_decay(mask, b[:, :, None] - b[:, None, :])
        scores = jnp.sum(qc[:, :, None] * kc[:, None] * rel, axis=-1)
        y = (jnp.einsum("btsh,bshv->bthv", scores, vc)
             + jnp.einsum("bthk,bhkv->bthv", qc * jnp.exp(b), S))
        b_last = b[:, -1]
        S = (jnp.exp(b_last)[..., None] * S
             + jnp.einsum("bshk,bshv->bhkv", kc * jnp.exp(b_last[:, None] - b), vc))
        return S, y

    S0 = jnp.zeros((nb, H, K, V), jnp.float32)
    _, ys = lax.scan(step, S0, (to_chunks(q), to_chunks(k), to_chunks(v), to_chunks(log_f)))
    return jnp.moveaxis(ys, 0, 1).reshape(nb, L, H, V).astype(v.dtype)


def decay_chunk_scan(q, k, v, log_a, chunk):
    nb, L, G, N = q.shape
    E, P = v.shape[-2:]
    nc = L // chunk
    to_chunks = lambda a: jnp.moveaxis(a.astype(jnp.float32).reshape(nb, nc, chunk, *a.shape[2:]), 1, 0)
    mask = jnp.tril(jnp.ones((chunk, chunk), bool))

    def step(S, inp):
        qc, kc, vc, lc = inp
        b = jnp.cumsum(lc, axis=1)
        bt = jnp.moveaxis(b, 1, -1)
        rel = masked_decay(mask, bt[..., :, None] - bt[..., None, :])
        qk = jnp.einsum("btgn,bsgn->bgts", qc, kc)
        y = (jnp.einsum("bgets,bsgep->btgep", qk[:, :, None] * rel, vc)
             + jnp.einsum("btgn,bgenp->btgep", qc, S) * jnp.exp(b)[..., None])
        b_last = b[:, -1]
        S = (jnp.exp(b_last)[..., None, None] * S
             + jnp.einsum("bsgn,bsgep->bgenp", kc, vc * jnp.exp(b_last[:, None] - b)[..., None]))
        return S, y

    S0 = jnp.zeros((nb, G, E, N, P), jnp.float32)
    _, ys = lax.scan(step, S0, (to_chunks(q), to_chunks(k), to_chunks(v), to_chunks(log_a)))
    return jnp.moveaxis(ys, 0, 1).reshape(nb, L, G, E, P).astype(v.dtype)


def lin_combine(left, right):
    a1, b1 = left
    a2, b2 = right
    return a1 * a2, a2 * b1 + b2


def apply_rope(t, cos, sin):
    half = t.shape[-1] // 2
    t1, t2 = t[..., :half], t[..., half:]
    cs, sn = cos[None, :, None, :], sin[None, :, None, :]
    return jnp.concatenate([t1 * cs - t2 * sn, t1 * sn + t2 * cs], axis=-1)


def adaln(cond, w, b):
    return jnp.split(jax.nn.silu(cond) @ w + b, 6, axis=-1)


def hybrid_mixer(h, w_in, lb, hgrn_norm_w, ssm_conv_w, ssm_conv_b, ssm_dt_bias, ssm_a_log, ssm_d, ssm_norm_w,
                 lru_conv_w, lru_conv_b, lru_wa, lru_ba, lru_wi, lru_bi, lru_lambda, ret_log_gamma,
                 cos, sin, w_out):
    nb, L, _ = h.shape
    w_parts = jnp.split(w_in, np.cumsum(IN_SPLITS)[:-1].tolist(), axis=1)
    (a_q, a_i, a_fw, a_bw, a_g, b_z, b_xbc, b_dt, c_x, c_g,
     d_q, d_k, d_v, d_g) = [h @ wp for wp in w_parts]

    u = dir_stack(a_fw, a_bw)
    lb_s = jnp.repeat(lb, nb, axis=0)[:, None, :]
    log_f = jnp.logaddexp(jnp.log(jnp.maximum(lb_s, LB_FLOOR)), jnp.log1p(-lb_s) + jax.nn.log_sigmoid(u))
    k_a = (1.0 - lb_s) * jax.nn.sigmoid(-u)
    hd = lambda t: t.reshape(2 * nb, L, A_HEADS, A_HEAD)
    o_a = gla_chunk_scan(hd(dir_stack(a_q, a_q)), hd(k_a), hd(dir_stack(a_i, a_i)), hd(log_f), A_CHUNK)
    o_a = rms_norm(dir_merge(o_a), hgrn_norm_w.reshape(A_HEADS, A_HEAD)).reshape(nb, L, GROUP_W)
    out_a = o_a * jax.nn.silu(a_g)

    xbc = jax.nn.silu(seg_dwconv(b_xbc, ssm_conv_w, ssm_conv_b, CONV4_PAD))
    bx, bB, bC = jnp.split(xbc, [GROUP_W, GROUP_W + B_GROUPS * B_STATE], axis=-1)
    hpg = B_HEADS // B_GROUPS
    dt = jax.nn.softplus(dir_stack(b_dt[..., :B_HEADS], b_dt[..., B_HEADS:])
                         + jnp.repeat(ssm_dt_bias, nb, axis=0)[:, None, :])
    log_a_b = dt * jnp.repeat(-jnp.exp(ssm_a_log), nb, axis=0)[:, None, :]
    xs_b = dir_stack(bx, bx).reshape(2 * nb, L, B_GROUPS, hpg, B_HEADDIM)
    gs = lambda t: dir_stack(t, t).reshape(2 * nb, L, B_GROUPS, B_STATE)
    y_b = decay_chunk_scan(gs(bC), gs(bB), xs_b * dt.reshape(2 * nb, L, B_GROUPS, hpg)[..., None],
                           log_a_b.reshape(2 * nb, L, B_GROUPS, hpg), SCAN_CHUNK)
    y_b = (dir_merge(y_b).reshape(nb, L, B_HEADS, B_HEADDIM)
           + ssm_d[:, None] * bx.reshape(nb, L, B_HEADS, B_HEADDIM))
    y_b = (y_b.reshape(nb, L, GROUP_W) * jax.nn.silu(b_z)).reshape(nb, L, B_GROUPS, GROUP_W // B_GROUPS)
    out_b = rms_norm(y_b, ssm_norm_w.reshape(B_GROUPS, -1)).reshape(nb, L, GROUP_W)

    cx = seg_dwconv(c_x, lru_conv_w, lru_conv_b, CONV4_PAD)
    xs_c = dir_stack(cx, cx).reshape(2, nb, L, C_BLOCKS, C_BLOCK)
    r = jax.nn.sigmoid(jnp.einsum("dblhi,dhij->dblhj", xs_c, lru_wa).reshape(2, nb, L, GROUP_W)
                       + lru_ba[:, None, None])
    gi = jax.nn.sigmoid(jnp.einsum("dblhi,dhij->dblhj", xs_c, lru_wi).reshape(2, nb, L, GROUP_W)
                        + lru_bi[:, None, None])
    log_a_c = -C_POW * r * jax.nn.softplus(-lru_lambda)[:, None, None]
    b_in = jnp.sqrt(-jnp.expm1(2.0 * log_a_c)) * gi * xs_c.reshape(2, nb, L, GROUP_W)
    _, h_c = lax.associative_scan(lin_combine, (jnp.exp(log_a_c), b_in), axis=2)
    out_c = dir_merge(h_c.reshape(2 * nb, L, GROUP_W)) * jax.nn.gelu(c_g)

    qd = apply_rope(d_q.reshape(nb, L, D_HEADS, D_HEAD), cos, sin)
    kd = apply_rope(d_k.reshape(nb, L, D_HEADS, D_HEAD), cos, sin) * D_HEAD ** -0.5
    vd = d_v.reshape(nb, L, D_HEADS, 1, D_HEAD)
    log_g = jnp.broadcast_to(jnp.repeat(ret_log_gamma, nb, axis=0)[:, None, :, None], (2 * nb, L, D_HEADS, 1))
    y_d = decay_chunk_scan(dir_stack(qd, qd), dir_stack(kd, kd), dir_stack(vd, vd), log_g, SCAN_CHUNK)
    out_d = head_norm(dir_merge(y_d)[..., 0, :]).reshape(nb, L, GROUP_W) * jax.nn.silu(d_g)

    return jnp.concatenate([out_a, out_b, out_c, out_d], axis=-1) @ w_out


def conv_ffn(h, w_in, conv_w, conv_b, w_out):
    gate, up = jnp.split(h @ w_in, 2, axis=-1)
    gate = dwconv(gate, conv_w, conv_b, (1, 1))
    return (jax.nn.gelu(gate) * up) @ w_out


def setup_inputs(seed: int = 0) -> dict:
    key = jax.random.key(seed)
    ks = iter(jax.random.split(key, 40))
    f32 = jnp.float32
    nrm = lambda shape, s: s * jax.random.normal(next(ks), shape, f32)
    x = nrm((BATCH, SEQ, D_MODEL), 1.0)
    c = nrm((BATCH, D_MODEL), 1.0)
    ctx = nrm((BATCH, CTX_LEN, D_MODEL), 1.0)
    c_ctx = nrm((D_MODEL,), 1.0)
    ada_w = nrm((DEPTH, D_MODEL, 6 * D_MODEL), D_MODEL ** -0.5)
    ada_b = nrm((DEPTH, 6 * D_MODEL), 0.02)
    w_in = nrm((DEPTH, D_MODEL, D_IN), D_MODEL ** -0.5)
    hgrn_lb_logits = nrm((DEPTH, 2, GROUP_W), 0.1)
    hgrn_norm_w = 1.0 + nrm((DEPTH, GROUP_W), 0.02)
    ssm_conv_w = nrm((DEPTH, B_CONV, XBC_W), B_CONV ** -0.5)
    ssm_conv_b = nrm((DEPTH, XBC_W), 0.02)
    dt0 = jnp.exp(jax.random.uniform(next(ks), (DEPTH, 2, B_HEADS), f32, math.log(1e-3), math.log(1e-1)))
    ssm_dt_bias = dt0 + jnp.log(-jnp.expm1(-dt0))
    ssm_a_log = jnp.log(jax.random.uniform(next(ks), (DEPTH, 2, B_HEADS), f32, 1.0, 16.0))
    ssm_d = 1.0 + nrm((DEPTH, B_HEADS), 0.1)
    ssm_norm_w = 1.0 + nrm((DEPTH, GROUP_W), 0.02)
    lru_conv_w = nrm((DEPTH, C_CONV, GROUP_W), C_CONV ** -0.5)
    lru_conv_b = nrm((DEPTH, GROUP_W), 0.02)
    lru_wa = nrm((DEPTH, 2, C_BLOCKS, C_BLOCK, C_BLOCK), C_BLOCK ** -0.5)
    lru_ba = nrm((DEPTH, 2, GROUP_W), 0.02)
    lru_wi = nrm((DEPTH, 2, C_BLOCKS, C_BLOCK, C_BLOCK), C_BLOCK ** -0.5)
    lru_bi = nrm((DEPTH, 2, GROUP_W), 0.02)
    a_c = jax.random.uniform(next(ks), (DEPTH, 2, GROUP_W), f32, 0.9, 0.999)
    s = a_c ** (1.0 / C_POW)
    lru_lambda = jnp.log(s) - jnp.log1p(-s)
    ret_decay_logit = (jnp.log(2.0 ** (5.0 + jnp.arange(D_HEADS, dtype=f32)) - 1.0)
                       + nrm((DEPTH, 2, D_HEADS), 0.1))
    w_out = nrm((DEPTH, D_MIX, D_MODEL), BETA * D_MIX ** -0.5)
    ln1_g = 1.0 + nrm((DEPTH, D_MODEL), 0.02)
    ln1_b = nrm((DEPTH, D_MODEL), 0.02)
    ffn_w_in = nrm((DEPTH, D_MODEL, 2 * D_FF), D_MODEL ** -0.5)
    ffn_conv_w = nrm((DEPTH, FFN_CONV, D_FF), FFN_CONV ** -0.5)
    ffn_conv_b = nrm((DEPTH, D_FF), 0.02)
    ffn_w_out = nrm((DEPTH, D_FF, D_MODEL), BETA * D_FF ** -0.5)
    ln2_g = 1.0 + nrm((DEPTH, D_MODEL), 0.02)
    ln2_b = nrm((DEPTH, D_MODEL), 0.02)
    return {"x": x, "c": c, "ctx": ctx, "c_ctx": c_ctx, "ada_w": ada_w, "ada_b": ada_b, "w_in": w_in,
            "hgrn_lb_logits": hgrn_lb_logits, "hgrn_norm_w": hgrn_norm_w,
            "ssm_conv_w": ssm_conv_w, "ssm_conv_b": ssm_conv_b, "ssm_dt_bias": ssm_dt_bias,
            "ssm_a_log": ssm_a_log, "ssm_d": ssm_d, "ssm_norm_w": ssm_norm_w,
            "lru_conv_w": lru_conv_w, "lru_conv_b": lru_conv_b, "lru_wa": lru_wa, "lru_ba": lru_ba,
            "lru_wi": lru_wi, "lru_bi": lru_bi, "lru_lambda": lru_lambda, "ret_decay_logit": ret_decay_logit,
            "w_out": w_out, "ln1_g": ln1_g, "ln1_b": ln1_b, "ffn_w_in": ffn_w_in, "ffn_conv_w": ffn_conv_w,
            "ffn_conv_b": ffn_conv_b, "ffn_w_out": ffn_w_out, "ln2_g": ln2_g, "ln2_b": ln2_b}


def reference(x, c, ctx, c_ctx, ada_w, ada_b, w_in, hgrn_lb_logits, hgrn_norm_w,
              ssm_conv_w, ssm_conv_b, ssm_dt_bias, ssm_a_log, ssm_d, ssm_norm_w,
              lru_conv_w, lru_conv_b, lru_wa, lru_ba, lru_wi, lru_bi, lru_lambda, ret_decay_logit,
              w_out, ln1_g, ln1_b, ffn_w_in, ffn_conv_w, ffn_conv_b, ffn_w_out, ln2_g, ln2_b):
    n = x.shape[1]
    ROWS = n // GRID_W
    row = jnp.repeat(jnp.arange(ROWS), GRID_W)
    col = jnp.tile(jnp.arange(GRID_W), ROWS)
    n_freq = D_HEAD // 4
    inv = ROPE_BASE ** (-jnp.arange(n_freq, dtype=jnp.float32) / n_freq)
    ang = jnp.concatenate([row[:, None] * inv, col[:, None] * inv], axis=-1)
    ang = jnp.concatenate([jnp.zeros((CTX_LEN, D_HEAD // 2), jnp.float32), ang], axis=0)
    cos, sin = jnp.cos(ang).astype(x.dtype), jnp.sin(ang).astype(x.dtype)
    p = jax.nn.softmax(hgrn_lb_logits.astype(jnp.float32), axis=0)
    lb_all = jnp.cumsum(p, axis=0) - p
    ret_log_gamma = jax.nn.log_sigmoid(ret_decay_logit.astype(jnp.float32))

    h_lat, h_ctx = x, ctx
    for l in range(DEPTH):
        sh1, sc1, g1, sh2, sc2, g2 = [m[:, None] for m in adaln(c, ada_w[l], ada_b[l])]
        csh1, csc1, cg1, csh2, csc2, cg2 = adaln(c_ctx, ada_w[l], ada_b[l])
        u = jnp.concatenate([h_ctx * (1.0 + csc1) + csh1, h_lat * (1.0 + sc1) + sh1], axis=1)
        mix = hybrid_mixer(u, w_in[l], lb_all[l], hgrn_norm_w[l], ssm_conv_w[l], ssm_conv_b[l],
                           ssm_dt_bias[l], ssm_a_log[l], ssm_d[l], ssm_norm_w[l],
                           lru_conv_w[l], lru_conv_b[l], lru_wa[l], lru_ba[l], lru_wi[l], lru_bi[l],
                           lru_lambda[l], ret_log_gamma[l], cos, sin, w_out[l])
        h_lat = layer_norm(ALPHA * h_lat + g1 * mix[:, CTX_LEN:], ln1_g[l], ln1_b[l])
        f_lat = conv_ffn(h_lat * (1.0 + sc2) + sh2, ffn_w_in[l], ffn_conv_w[l], ffn_conv_b[l], ffn_w_out[l])
        h_lat = layer_norm(ALPHA * h_lat + g2 * f_lat, ln2_g[l], ln2_b[l])
        if l < DEPTH - 1:
            h_ctx = layer_norm(ALPHA * h_ctx + cg1 * mix[:, :CTX_LEN], ln1_g[l], ln1_b[l])
            f_ctx = conv_ffn(h_ctx * (1.0 + csc2) + csh2, ffn_w_in[l], ffn_conv_w[l], ffn_conv_b[l], ffn_w_out[l])
            h_ctx = layer_norm(ALPHA * h_ctx + cg2 * f_ctx, ln2_g[l], ln2_b[l])
    return h_lat
```

```python
import functools
import math

import jax
import jax.numpy as jnp
from jax import lax
from jax.experimental import pallas as pl
from jax.experimental.pallas import tpu as pltpu

F32 = jnp.float32
BF16 = jnp.bfloat16

D_MODEL = 1024
DEPTH = 4
GRID_W = 64
CTX_LEN = 256
GROUP_W = 512
A_HEAD = 128
A_HEADS = 4
LB_FLOOR = 1e-30
B_HEADDIM = 64
B_HEADS = 8
B_GROUPS = 2
B_STATE = 128
XBC_W = GROUP_W + 2 * B_GROUPS * B_STATE
C_BLOCKS = 8
C_BLOCK = 64
C_POW = 8.0
D_HEAD = 128
D_HEADS = 4
ROPE_BASE = 10000.0
D_FF = 2816
ALPHA = (2 * DEPTH) ** 0.25
EPS = 1e-6

TM = 256
HALO = 16
FFN_HALO = 8
A_CHUNK = 64
A_SUB = 8
B_CHUNK = 128
D_CHUNK = 128
LRU_SEG = 8
FF_CHUNK = 256
VMEM_LIMIT = 56 * 1024 * 1024

COL_AQ, COL_AI, COL_AF, COL_AG = 0, 512, 1024, 2048
COL_BZ, COL_BXBC = 2560, 3072
COL_CX, COL_CG = 4096, 4608
COL_DQ, COL_DK, COL_DV, COL_DG = 5120, 5632, 6144, 6656
COL_DT = 7168
P_W = 7296
DT_W = 128


def _cparams(sem):
    return pltpu.CompilerParams(dimension_semantics=sem, vmem_limit_bytes=VMEM_LIMIT)


def _softplus(x):
    return jnp.maximum(x, 0.0) + jnp.log1p(jnp.exp(-jnp.abs(x)))


def _sigmoid(x):
    return 1.0 / (1.0 + jnp.exp(-x))


def _silu(x):
    return x * _sigmoid(x)


def _gelu_tanh(x):
    return 0.5 * x * (1.0 + jnp.tanh(math.sqrt(2.0 / math.pi) * (x + 0.044715 * (x * x * x))))


def _dot(a, b):
    return jnp.dot(a, b, preferred_element_type=F32)


def _dot_nt(a, b):
    return lax.dot_general(a, b, (((1,), (1,)), ((), ())), preferred_element_type=F32)


def _split3(x):
    hi = x.astype(BF16)
    r1 = x - hi.astype(F32)
    mid = r1.astype(BF16)
    lo = (r1 - mid.astype(F32)).astype(BF16)
    return hi, mid, lo


def _sel_dot(m01, x):
    hi, mid, lo = _split3(x)
    return _dot(m01, hi) + _dot(m01, mid) + _dot(m01, lo)


def _seq_block(d, j, nblk):
    return jnp.where(d == 0, j, jnp.where(j == 0, 0, nblk - j))


def _ada_kernel(c_ref, w_ref, b_ref, o_ref):
    s = _silu(c_ref[...])
    o_ref[0] = jnp.dot(s, w_ref[0], preferred_element_type=F32,
                       precision=lax.Precision.HIGHEST) + b_ref[0]


def _ada_call(cond, ada_w, ada_b):
    depth, d, n6 = ada_w.shape
    rows = cond.shape[0]
    tn = 1536
    return pl.pallas_call(
        _ada_kernel,
        grid=(depth, n6 // tn),
        in_specs=[pl.BlockSpec((rows, d), lambda l, n: (0, 0)),
                  pl.BlockSpec((1, d, tn), lambda l, n: (l, 0, n)),
                  pl.BlockSpec((1, 1, tn), lambda l, n: (l, 0, n))],
        out_specs=pl.BlockSpec((1, rows, tn), lambda l, n: (l, 0, n)),
        out_shape=jax.ShapeDtypeStruct((depth, rows, n6), F32),
        compiler_params=_cparams(("parallel", "parallel")),
        name="ada_mod",
    )(cond, ada_w, ada_b.reshape(depth, 1, n6))


def _inproj_kernel(h_ref, mod_ref, w_ref, o_ref, *, nb, tn):
    m = mod_ref[:, 0]
    u = h_ref[...] * (1.0 + m[:, 1:2, :]) + m[:, 0:1, :]
    ub = u.astype(BF16).reshape(nb * TM, D_MODEL)
    for n0 in range(0, P_W, tn):
        w = min(tn, P_W - n0)
        o_ref[:, :, n0:n0 + w] = _dot(ub, w_ref[:, n0:n0 + w]).astype(BF16).reshape(nb, TM, w)


def _inproj_call(h, mod, w_in_b, nb):
    b, l, d = h.shape
    kern = functools.partial(_inproj_kernel, nb=nb, tn=512)
    return pl.pallas_call(
        kern,
        grid=(b // nb, l // TM),
        in_specs=[pl.BlockSpec((nb, TM, d), lambda i, t: (i, t, 0)),
                  pl.BlockSpec((nb, 1, 6, d), lambda i, t: (i, jnp.minimum(t, 1), 0, 0)),
                  pl.BlockSpec((d, P_W), lambda i, t: (0, 0), pipeline_mode=pl.Buffered(1))],
        out_specs=pl.BlockSpec((nb, TM, P_W), lambda i, t: (i, t, 0)),
        out_shape=jax.ShapeDtypeStruct((b, l, P_W), BF16),
        compiler_params=_cparams(("parallel", "parallel")),
        name="in_proj",
    )(h, mod, w_in_b)


def _mixa_body(q_ref, v_ref, f_ref, lb_ref, y_ref, st_ref, sh_ref, rev):
    C, m = A_CHUNK, A_SUB
    nsl = C // m - 1
    lbv = lb_ref[0]
    log_lb = jnp.log(jnp.maximum(lbv, LB_FLOOR))
    log_1mlb = jnp.log1p(-lbv)
    one_m_lb = 1.0 - lbv

    row = lax.broadcasted_iota(jnp.int32, (C, C), 0)
    col = lax.broadcasted_iota(jnp.int32, (C, C), 1)
    tau_r = (C - 1 - row) if rev else row
    tau_c = (C - 1 - col) if rev else col
    blk_r, blk_c = tau_r // m, tau_c // m
    tri = (tau_c <= tau_r).astype(BF16)
    refrow = (C - m * blk_r) if rev else (m * blk_r - 1)
    rsel = ((col == refrow) & (blk_r >= 1)).astype(BF16)
    off_mask = blk_c < blk_r
    r1 = lax.broadcasted_iota(jnp.int32, (C, 1), 0)
    tau1 = (C - 1 - r1) if rev else r1
    blk1 = tau1 // m
    off1 = tau1 % m

    pad = m
    for c in (range(TM // C - 1, -1, -1) if rev else range(TM // C)):
        sl = pl.ds(c * C, C)
        q = q_ref[0, sl, :].astype(F32)
        v = v_ref[0, sl, :].astype(F32)
        u = f_ref[0, sl, :].astype(F32)
        log_sig = jnp.minimum(u, 0.0) - jnp.log1p(jnp.exp(-jnp.abs(u)))
        t2 = log_1mlb + log_sig
        log_f = jnp.maximum(log_lb, t2) + jnp.log1p(jnp.exp(-jnp.abs(log_lb - t2)))
        k = one_m_lb * _sigmoid(-u)
        b = _sel_dot(tri, log_f)
        rr = _sel_dot(rsel, b)
        qt = q * jnp.exp(jnp.minimum(b - rr, 0.0))
        sh_ref[0, pl.ds(pad, C), :] = k
        sh_ref[1, pl.ds(pad, C), :] = b
        sh_ref[2, pl.ds(pad, C), :] = v
        y_parts = []
        for h in range(A_HEADS):
            hs = slice(h * A_HEAD, (h + 1) * A_HEAD)
            qh, kh, vh, bh, qth = q[:, hs], k[:, hs], v[:, hs], b[:, hs], qt[:, hs]
            tot = bh[0:1, :] if rev else bh[C - 1:C, :]
            q_aug = jnp.concatenate(
                [jnp.where(blk1 == i, qth, 0.0) for i in range(1, nsl + 1)], axis=1).astype(BF16)
            k_parts = []
            for i in range(1, nsl + 1):
                rrow = (C - m * i) if rev else (m * i - 1)
                k_parts.append(kh * jnp.exp(jnp.minimum(bh[rrow:rrow + 1, :] - bh, 0.0)))
            k_aug = jnp.concatenate(k_parts, axis=1).astype(BF16)
            p_off = jnp.where(off_mask, _dot_nt(q_aug, k_aug), 0.0)
            y = _dot(p_off.astype(BF16), vh.astype(BF16))
            for dl in range(m):
                if dl == 0:
                    ks, bs, vs = kh, bh, vh
                else:
                    st = pad + dl if rev else pad - dl
                    ks = sh_ref[0, pl.ds(st, C), hs]
                    bs = sh_ref[1, pl.ds(st, C), hs]
                    vs = sh_ref[2, pl.ds(st, C), hs]
                tmp = qh * ks * jnp.exp(jnp.minimum(bh - bs, 0.0))
                p = jnp.sum(tmp, axis=-1, keepdims=True)
                y = y + jnp.where(off1 >= dl, p, 0.0) * vs
            st_t = st_ref[h]
            y = y + _dot_nt((qh * jnp.exp(bh)).astype(BF16), st_t.astype(BF16))
            kdec = (kh * jnp.exp(tot - bh)).astype(BF16)
            st_ref[h] = st_t * jnp.exp(tot) + _dot(vh.T.astype(BF16), kdec)
            y_parts.append(y)
        y_ref[0, 0, sl, :] = jnp.concatenate(y_parts, axis=1).astype(BF16)


def _mixa_kernel(q_ref, v_ref, f_ref, lb_ref, y_ref, st_ref, sh_ref):
    d = pl.program_id(1)
    j = pl.program_id(2)

    @pl.when(j == 0)
    def _():
        st_ref[...] = jnp.zeros_like(st_ref)
        sh_ref[...] = jnp.zeros_like(sh_ref)

    @pl.when(d == 0)
    def _():
        _mixa_body(q_ref, v_ref, f_ref, lb_ref, y_ref, st_ref, sh_ref, False)

    @pl.when(d == 1)
    def _():
        _mixa_body(q_ref, v_ref, f_ref, lb_ref, y_ref, st_ref, sh_ref, True)


def _mixa_call(p, lb):
    b, l, _ = p.shape
    nblk = l // TM
    cb = lambda col: col // GROUP_W
    rowmap = lambda col: (lambda i, d, j: (i, _seq_block(d, j, nblk), cb(col)))
    return pl.pallas_call(
        _mixa_kernel,
        grid=(b, 2, nblk),
        in_specs=[pl.BlockSpec((1, TM, GROUP_W), rowmap(COL_AQ)),
                  pl.BlockSpec((1, TM, GROUP_W), rowmap(COL_AI)),
                  pl.BlockSpec((1, TM, GROUP_W), lambda i, d, j: (i, _seq_block(d, j, nblk), cb(COL_AF) + d)),
                  pl.BlockSpec((1, 1, GROUP_W), lambda i, d, j: (d, 0, 0))],
        out_specs=pl.BlockSpec((1, 1, TM, GROUP_W), lambda i, d, j: (d, i, _seq_block(d, j, nblk), 0)),
        out_shape=jax.ShapeDtypeStruct((2, b, l, GROUP_W), BF16),
        scratch_shapes=[pltpu.VMEM((A_HEADS, A_HEAD, A_HEAD), F32),
                        pltpu.VMEM((3, A_CHUNK + 2 * A_SUB, GROUP_W), F32)],
        compiler_params=_cparams(("parallel", "parallel", "arbitrary")),
        name="mix_hgrn2",
    )(p, p, p, lb.reshape(2, 1, GROUP_W))


def _conv4(x_ref, xp_ref, xn_ref, w_ref, b_ref, xs_ref, rb, nblk):
    prev_ok = rb >= 2
    next_ok = (rb >= 1) & (rb <= nblk - 2)
    xs_ref[pl.ds(0, HALO), :] = jnp.where(prev_ok, xp_ref[0].astype(F32), 0.0)
    xs_ref[pl.ds(HALO, TM), :] = x_ref[0].astype(F32)
    xs_ref[pl.ds(HALO + TM, HALO), :] = jnp.where(next_ok, xn_ref[0].astype(F32), 0.0)
    acc = b_ref[...] + w_ref[0:1, :] * xs_ref[pl.ds(HALO - 2, TM), :]
    for jj in range(1, 4):
        acc = acc + w_ref[jj:jj + 1, :] * xs_ref[pl.ds(HALO - 2 + jj, TM), :]
    return acc


def _halo_specs(width, colblk, nblk, l):
    per = TM // HALO
    last = l // HALO - 1
    prev = pl.BlockSpec((1, HALO, width),
                        lambda i, d, j: (i, jnp.maximum(_seq_block(d, j, nblk) * per - 1, 0), colblk))
    nxt = pl.BlockSpec((1, HALO, width),
                       lambda i, d, j: (i, jnp.minimum((_seq_block(d, j, nblk) + 1) * per, last), colblk))
    return prev, nxt


def _mixb_body(dt_ref, dtb_ref, alog_ref, e_ref, dsk_ref, y_ref, st_ref, xc_ref, rev, d):
    C = B_CHUNK
    hpg = B_HEADS // B_GROUPS
    gw = hpg * B_HEADDIM
    row = lax.broadcasted_iota(jnp.int32, (C, C), 0)
    col = lax.broadcasted_iota(jnp.int32, (C, C), 1)
    trimask = (col >= row) if rev else (col <= row)
    tri = trimask.astype(BF16)
    lane_head = lax.broadcasted_iota(jnp.int32, (1, gw), 1) // B_HEADDIM
    e01 = e_ref[0]
    dtb = dtb_ref[0]
    neg_a = -jnp.exp(alog_ref[0])

    def expand(x):
        hi, mid, lo = _split3(x)
        return _dot(hi, e01) + _dot(mid, e01) + _dot(lo, e01)

    for c in (range(TM // C - 1, -1, -1) if rev else range(TM // C)):
        sl = pl.ds(c * C, C)
        bx = xc_ref[sl, 0:GROUP_W]
        bb = xc_ref[sl, GROUP_W:GROUP_W + B_GROUPS * B_STATE]
        bc = xc_ref[sl, GROUP_W + B_GROUPS * B_STATE:XBC_W]
        dt = _softplus(dt_ref[0, sl, :].astype(F32) + dtb)
        la = dt * neg_a
        b = _sel_dot(tri, la)
        b_exp = expand(b)
        dt_exp = expand(dt)
        tot_exp = b_exp[0:1, :] if rev else b_exp[C - 1:C, :]
        v_all = bx * dt_exp
        vdec = v_all * jnp.exp(tot_exp - b_exp)
        eb = jnp.exp(b_exp)
        bt = b.T
        y_parts = []
        for g in range(B_GROUPS):
            gs = slice(g * gw, (g + 1) * gw)
            bg = bb[:, g * B_STATE:(g + 1) * B_STATE]
            cg = bc[:, g * B_STATE:(g + 1) * B_STATE].astype(BF16)
            qk = _dot_nt(cg, bg.astype(BF16))
            s_g = st_ref[g]
            y_g = _dot(cg, s_g.astype(BF16)) * eb[:, gs]
            vg = v_all[:, gs].astype(BF16)
            for e in range(hpg):
                lane = 8 * d + g * hpg + e
                rel = jnp.where(trimask,
                                jnp.exp(jnp.minimum(b[:, lane:lane + 1] - bt[lane:lane + 1, :], 0.0)), 0.0)
                r = _dot((qk * rel).astype(BF16), vg)
                y_g = y_g + jnp.where(lane_head == e, r, 0.0)
            st_ref[g] = s_g * jnp.exp(tot_exp[:, gs]) + _dot(bg.T.astype(BF16), vdec[:, gs].astype(BF16))
            y_parts.append(y_g)
        y = jnp.concatenate(y_parts, axis=1)
        if not rev:
            y = y + dsk_ref[...] * bx
        y_ref[0, 0, sl, :] = y.astype(BF16)


def _mixb_kernel(x_ref, xp_ref, xn_ref, dt_ref, cw_ref, cb_ref, dtb_ref, alog_ref, e_ref, dsk_ref,
                 y_ref, st_ref, xs_ref, xc_ref, *, nblk):
    d = pl.program_id(1)
    j = pl.program_id(2)
    rb = _seq_block(d, j, nblk)

    @pl.when(j == 0)
    def _():
        st_ref[...] = jnp.zeros_like(st_ref)

    xc_ref[...] = _silu(_conv4(x_ref, xp_ref, xn_ref, cw_ref, cb_ref, xs_ref, rb, nblk))

    @pl.when(d == 0)
    def _():
        _mixb_body(dt_ref, dtb_ref, alog_ref, e_ref, dsk_ref, y_ref, st_ref, xc_ref, False, 0)

    @pl.when(d == 1)
    def _():
        _mixb_body(dt_ref, dtb_ref, alog_ref, e_ref, dsk_ref, y_ref, st_ref, xc_ref, True, 1)


def _mixb_call(p, conv_w, conv_b, dt_bias, a_log, ssm_d):
    b, l, _ = p.shape
    nblk = l // TM
    pad_dir = lambda a: jnp.stack([jnp.pad(a[0], (0, DT_W - B_HEADS)),
                                   jnp.pad(a[1], (B_HEADS, DT_W - 2 * B_HEADS))]).reshape(2, 1, DT_W)
    lanes = jnp.arange(GROUP_W) // B_HEADDIM
    e01 = jnp.stack([(jnp.arange(DT_W)[:, None] == lanes[None, :] + 8 * dd) for dd in range(2)]).astype(BF16)
    dsk = jnp.repeat(ssm_d, B_HEADDIM).reshape(1, GROUP_W)
    xcol = COL_BXBC // XBC_W
    prev, nxt = _halo_specs(XBC_W, xcol, nblk, l)
    return pl.pallas_call(
        functools.partial(_mixb_kernel, nblk=nblk),
        grid=(b, 2, nblk),
        in_specs=[pl.BlockSpec((1, TM, XBC_W), lambda i, d, j: (i, _seq_block(d, j, nblk), xcol)),
                  prev, nxt,
                  pl.BlockSpec((1, TM, DT_W), lambda i, d, j: (i, _seq_block(d, j, nblk), COL_DT // DT_W)),
                  pl.BlockSpec((4, XBC_W), lambda i, d, j: (0, 0)),
                  pl.BlockSpec((1, XBC_W), lambda i, d, j: (0, 0)),
                  pl.BlockSpec((1, 1, DT_W), lambda i, d, j: (d, 0, 0)),
                  pl.BlockSpec((1, 1, DT_W), lambda i, d, j: (d, 0, 0)),
                  pl.BlockSpec((1, DT_W, GROUP_W), lambda i, d, j: (d, 0, 0)),
                  pl.BlockSpec((1, GROUP_W), lambda i, d, j: (0, 0))],
        out_specs=pl.BlockSpec((1, 1, TM, GROUP_W), lambda i, d, j: (d, i, _seq_block(d, j, nblk), 0)),
        out_shape=jax.ShapeDtypeStruct((2, b, l, GROUP_W), BF16),
        scratch_shapes=[pltpu.VMEM((B_GROUPS, B_STATE, GROUP_W // B_GROUPS), F32),
                        pltpu.VMEM((TM + 2 * HALO, XBC_W), F32),
                        pltpu.VMEM((TM, XBC_W), F32)],
        compiler_params=_cparams(("parallel", "parallel", "arbitrary")),
        name="mix_ssd",
    )(p, p, p, p, conv_w, conv_b.reshape(1, XBC_W), pad_dir(dt_bias), pad_dir(a_log), e01, dsk)


def _mixc_body(y_ref, carry_ref, a_ref, b_ref, hl_ref, ac_ref, cm_ref, rev):
    nseg = LRU_SEG
    steps = TM // nseg
    ngrp = GROUP_W // 128
    for g in range(ngrp):
        h = jnp.zeros((nseg, 128), F32)
        acc = jnp.ones((nseg, 128), F32)
        for jj in (range(steps - 1, -1, -1) if rev else range(steps)):
            rows = pl.ds(jj, nseg, stride=steps)
            aj = a_ref[g, rows, :]
            h = aj * h + b_ref[g, rows, :]
            acc = aj * acc
            hl_ref[g, rows, :] = h
            ac_ref[g, rows, :] = acc
        cin = carry_ref[g]
        for i in (range(nseg - 1, -1, -1) if rev else range(nseg)):
            cm_ref[g, i:i + 1, :] = cin
            cin = acc[i:i + 1, :] * cin + h[i:i + 1, :]
        carry_ref[g] = cin
        cm = cm_ref[g]
        for jj in range(steps):
            rows = pl.ds(jj, nseg, stride=steps)
            b_ref[g, rows, :] = hl_ref[g, rows, :] + ac_ref[g, rows, :] * cm
    y_ref[0, 0] = jnp.concatenate([b_ref[g] for g in range(ngrp)], axis=1).astype(BF16)


def _mixc_kernel(x_ref, xp_ref, xn_ref, cw_ref, cb_ref, wa_ref, wi_ref, ba_ref, bi_ref, lam_ref,
                 y_ref, carry_ref, xs_ref, a_ref, b_ref, hl_ref, ac_ref, cm_ref, *, nblk):
    d = pl.program_id(1)
    j = pl.program_id(2)
    rb = _seq_block(d, j, nblk)

    @pl.when(j == 0)
    def _():
        carry_ref[...] = jnp.zeros_like(carry_ref)

    cx = _conv4(x_ref, xp_ref, xn_ref, cw_ref, cb_ref, xs_ref, rb, nblk)
    cxb = cx.astype(BF16)
    r = _sigmoid(_dot(cxb, wa_ref[0]) + ba_ref[0])
    gi = _sigmoid(_dot(cxb, wi_ref[0]) + bi_ref[0])
    la = (-C_POW) * r * _softplus(-lam_ref[0])
    a = jnp.exp(la)
    bin_ = jnp.sqrt(1.0 - jnp.exp(2.0 * la)) * gi * cx
    for g in range(GROUP_W // 128):
        a_ref[g] = a[:, g * 128:(g + 1) * 128]
        b_ref[g] = bin_[:, g * 128:(g + 1) * 128]

    @pl.when(d == 0)
    def _():
        _mixc_body(y_ref, carry_ref, a_ref, b_ref, hl_ref, ac_ref, cm_ref, False)

    @pl.when(d == 1)
    def _():
        _mixc_body(y_ref, carry_ref, a_ref, b_ref, hl_ref, ac_ref, cm_ref, True)


def _block_diag(w):
    nd, nb_, n, _ = w.shape
    eye = jnp.eye(nb_, dtype=w.dtype)
    return jnp.einsum("dhij,hg->dhigj", w, eye).reshape(nd, nb_ * n, nb_ * n)


def _mixc_call(p, conv_w, conv_b, wa, ba, wi, bi, lam):
    b, l, _ = p.shape
    nblk = l // TM
    xcol = COL_CX // GROUP_W
    prev, nxt = _halo_specs(GROUP_W, xcol, nblk, l)
    vec = lambda a: a.reshape(2, 1, GROUP_W)
    dspec = pl.BlockSpec((1, 1, GROUP_W), lambda i, d, j: (d, 0, 0))
    wspec = pl.BlockSpec((1, GROUP_W, GROUP_W), lambda i, d, j: (d, 0, 0))
    return pl.pallas_call(
        functools.partial(_mixc_kernel, nblk=nblk),
        grid=(b, 2, nblk),
        in_specs=[pl.BlockSpec((1, TM, GROUP_W), lambda i, d, j: (i, _seq_block(d, j, nblk), xcol)),
                  prev, nxt,
                  pl.BlockSpec((4, GROUP_W), lambda i, d, j: (0, 0)),
                  pl.BlockSpec((1, GROUP_W), lambda i, d, j: (0, 0)),
                  wspec, wspec, dspec, dspec, dspec],
        out_specs=pl.BlockSpec((1, 1, TM, GROUP_W), lambda i, d, j: (d, i, _seq_block(d, j, nblk), 0)),
        out_shape=jax.ShapeDtypeStruct((2, b, l, GROUP_W), BF16),
        scratch_shapes=[pltpu.VMEM((GROUP_W // 128, 1, 128), F32),
                        pltpu.VMEM((TM + 2 * HALO, GROUP_W), F32),
                        pltpu.VMEM((GROUP_W // 128, TM, 128), F32),
                        pltpu.VMEM((GROUP_W // 128, TM, 128), F32),
                        pltpu.VMEM((GROUP_W // 128, TM, 128), F32),
                        pltpu.VMEM((GROUP_W // 128, TM, 128), F32),
                        pltpu.VMEM((GROUP_W // 128, LRU_SEG, 128), F32)],
        compiler_params=_cparams(("parallel", "parallel", "arbitrary")),
        name="mix_rglru",
    )(p, p, p, conv_w, conv_b.reshape(1, GROUP_W), _block_diag(wa).astype(BF16),
      _block_diag(wi).astype(BF16), vec(ba), vec(bi), vec(lam))


def _mixd_body(q_ref, k_ref, v_ref, cos_ref, sin_ref, lgt_ref, y_ref, st_ref, rev):
    C = D_CHUNK
    lg = -_softplus(-lgt_ref[0])
    row = lax.broadcasted_iota(jnp.int32, (C, C), 0)
    col = lax.broadcasted_iota(jnp.int32, (C, C), 1)
    trimask = (col >= row) if rev else (col <= row)
    dist = jnp.maximum((col - row) if rev else (row - col), 0).astype(F32)
    r1 = lax.broadcasted_iota(jnp.int32, (C, 1), 0)
    tau1 = ((C - 1 - r1) if rev else r1).astype(F32)
    scale = D_HEAD ** -0.5
    for c in (range(TM // C - 1, -1, -1) if rev else range(TM // C)):
        sl = pl.ds(c * C, C)
        cosf = cos_ref[sl, :]
        sinf = sin_ref[sl, :]
        y_parts = []
        for h in range(D_HEADS):
            hs = slice(h * D_HEAD, (h + 1) * D_HEAD)
            lgh = lg[h:h + 1, 0:1]
            qh = q_ref[0, sl, hs].astype(F32)
            kh = k_ref[0, sl, hs].astype(F32)
            vh = v_ref[0, sl, hs]
            qr = qh * cosf + pltpu.roll(qh, D_HEAD // 2, 1) * sinf
            kr = (kh * cosf + pltpu.roll(kh, D_HEAD // 2, 1) * sinf) * scale
            qb = qr.astype(BF16)
            qk = _dot_nt(qb, kr.astype(BF16))
            rel = jnp.where(trimask, jnp.exp(lgh * dist), 0.0)
            s_h = st_ref[h]
            y = _dot((qk * rel).astype(BF16), vh) + _dot(qb, s_h.astype(BF16)) * jnp.exp(lgh * (tau1 + 1.0))
            kdec = kr * jnp.exp(lgh * (C - 1.0 - tau1))
            st_ref[h] = s_h * jnp.exp(lgh * float(C)) + _dot(kdec.T.astype(BF16), vh)
            y_parts.append(y)
        y_ref[0, 0, sl, :] = jnp.concatenate(y_parts, axis=1).astype(BF16)


def _mixd_kernel(q_ref, k_ref, v_ref, cos_ref, sin_ref, lgt_ref, y_ref, st_ref):
    d = pl.program_id(1)
    j = pl.program_id(2)

    @pl.when(j == 0)
    def _():
        st_ref[...] = jnp.zeros_like(st_ref)

    @pl.when(d == 0)
    def _():
        _mixd_body(q_ref, k_ref, v_ref, cos_ref, sin_ref, lgt_ref, y_ref, st_ref, False)

    @pl.when(d == 1)
    def _():
        _mixd_body(q_ref, k_ref, v_ref, cos_ref, sin_ref, lgt_ref, y_ref, st_ref, True)


def _mixd_call(p, cosf, sinf, decay_logit):
    b, l, _ = p.shape
    nblk = l // TM
    cb = lambda col: col // GROUP_W
    rowmap = lambda col: (lambda i, d, j: (i, _seq_block(d, j, nblk), cb(col)))
    lgt = jnp.broadcast_to(jnp.pad(decay_logit, ((0, 0), (0, 8 - D_HEADS)))[:, :, None], (2, 8, D_HEAD))
    tab = pl.BlockSpec((TM, D_HEAD), lambda i, d, j: (_seq_block(d, j, nblk), 0))
    return pl.pallas_call(
        _mixd_kernel,
        grid=(b, 2, nblk),
        in_specs=[pl.BlockSpec((1, TM, GROUP_W), rowmap(COL_DQ)),
                  pl.BlockSpec((1, TM, GROUP_W), rowmap(COL_DK)),
                  pl.BlockSpec((1, TM, GROUP_W), rowmap(COL_DV)),
                  tab, tab,
                  pl.BlockSpec((1, 8, D_HEAD), lambda i, d, j: (d, 0, 0))],
        out_specs=pl.BlockSpec((1, 1, TM, GROUP_W), lambda i, d, j: (d, i, _seq_block(d, j, nblk), 0)),
        out_shape=jax.ShapeDtypeStruct((2, b, l, GROUP_W), BF16),
        scratch_shapes=[pltpu.VMEM((D_HEADS, D_HEAD, D_HEAD), F32)],
        compiler_params=_cparams(("parallel", "parallel", "arbitrary")),
        name="mix_retention",
    )(p, p, p, cosf, sinf, lgt)


def _layer_norm_rows(x, g, b):
    mu = jnp.mean(x, axis=-1, keepdims=True)
    xc = x - mu
    var = jnp.mean(xc * xc, axis=-1, keepdims=True)
    return xc * lax.rsqrt(var + EPS) * g + b


def _outproj_kernel(yaf, yab, ybf, ybb, ycf, ycb, ydf, ydb, ag_ref, bz_ref, cg_ref, dg_ref,
                    h_ref, mod_ref, w_ref, nwa_ref, nwb_ref, lg_ref, lb_ref, o_ref, *, nb):
    rows = nb * TM
    ld = lambda f, bk: (f[0].astype(F32) + bk[0].astype(F32)).reshape(rows, GROUP_W)
    gate = lambda r: r[...].astype(F32).reshape(rows, GROUP_W)
    ya = ld(yaf, yab)
    parts = []
    for h in range(A_HEADS):
        x = ya[:, h * A_HEAD:(h + 1) * A_HEAD]
        parts.append(x * lax.rsqrt(jnp.mean(x * x, axis=-1, keepdims=True) + EPS))
    oa = jnp.concatenate(parts, axis=1) * nwa_ref[...] * _silu(gate(ag_ref))
    yb = ld(ybf, ybb) * _silu(gate(bz_ref))
    gw = GROUP_W // B_GROUPS
    parts = []
    for g in range(B_GROUPS):
        x = yb[:, g * gw:(g + 1) * gw]
        parts.append(x * lax.rsqrt(jnp.mean(x * x, axis=-1, keepdims=True) + EPS))
    ob = jnp.concatenate(parts, axis=1) * nwb_ref[...]
    oc = ld(ycf, ycb) * _gelu_tanh(gate(cg_ref))
    yd = ld(ydf, ydb)
    parts = []
    for h in range(D_HEADS):
        x = yd[:, h * D_HEAD:(h + 1) * D_HEAD]
        mu = jnp.mean(x, axis=-1, keepdims=True)
        xc = x - mu
        parts.append(xc * lax.rsqrt(jnp.mean(xc * xc, axis=-1, keepdims=True) + EPS))
    od = jnp.concatenate(parts, axis=1) * _silu(gate(dg_ref))
    mix = (_dot(oa.astype(BF16), w_ref[0:GROUP_W, :])
           + _dot(ob.astype(BF16), w_ref[GROUP_W:2 * GROUP_W, :])
           + _dot(oc.astype(BF16), w_ref[2 * GROUP_W:3 * GROUP_W, :])
           + _dot(od.astype(BF16), w_ref[3 * GROUP_W:4 * GROUP_W, :]))
    m = mod_ref[:, 0]
    hn = ALPHA * h_ref[...] + m[:, 2:3, :] * mix.reshape(nb, TM, D_MODEL)
    o_ref[...] = _layer_norm_rows(hn, lg_ref[...], lb_ref[...])


def _outproj_call(ya, yb, yc, yd, p, h, mod, w_out_b, nwa, nwb, ln_g, ln_b, nb):
    b, l, d = h.shape
    ydir = lambda dd: pl.BlockSpec((1, nb, TM, GROUP_W), lambda i, t: (dd, i, t, 0))
    pcol = lambda col: pl.BlockSpec((nb, TM, GROUP_W), lambda i, t: (i, t, col // GROUP_W))
    vec = lambda n: pl.BlockSpec((1, n), lambda i, t: (0, 0))
    return pl.pallas_call(
        functools.partial(_outproj_kernel, nb=nb),
        grid=(b // nb, l // TM),
        in_specs=[ydir(0), ydir(1), ydir(0), ydir(1), ydir(0), ydir(1), ydir(0), ydir(1),
                  pcol(COL_AG), pcol(COL_BZ), pcol(COL_CG), pcol(COL_DG),
                  pl.BlockSpec((nb, TM, d), lambda i, t: (i, t, 0)),
                  pl.BlockSpec((nb, 1, 6, d), lambda i, t: (i, jnp.minimum(t, 1), 0, 0)),
                  pl.BlockSpec((4 * GROUP_W, d), lambda i, t: (0, 0), pipeline_mode=pl.Buffered(1)),
                  vec(GROUP_W), vec(GROUP_W), vec(d), vec(d)],
        out_specs=pl.BlockSpec((nb, TM, d), lambda i, t: (i, t, 0)),
        out_shape=jax.ShapeDtypeStruct((b, l, d), F32),
        compiler_params=_cparams(("parallel", "parallel")),
        name="out_proj",
    )(ya, ya, yb, yb, yc, yc, yd, yd, p, p, p, p, h, mod, w_out_b,
      nwa.reshape(1, GROUP_W), nwb.reshape(1, GROUP_W), ln_g.reshape(1, d), ln_b.reshape(1, d))


def _ffn_kernel(h_ref, hp_ref, hn_ref, mod_ref, wg_ref, wu_ref, wo_ref, cw_ref, cb_ref, lg_ref, lb_ref,
                o_ref, acc_ref, g_ref, *, nb, nblk, first_blk):
    rb = pl.program_id(1) + first_blk
    prev_ok = rb >= 2
    next_ok = (rb >= 1) & (rb <= nblk - 2)
    m = mod_ref[:, 0]
    sh, sc = m[:, 3:4, :], m[:, 4:5, :]
    hm = h_ref[...]
    hrows = TM + 2 * FFN_HALO
    x_all = jnp.concatenate([hp_ref[...], hm, hn_ref[...]], axis=1) * (1.0 + sc) + sh
    xb_all = x_all.astype(BF16).reshape(nb * hrows, D_MODEL)
    xb = x_all[:, FFN_HALO:FFN_HALO + TM, :].astype(BF16).reshape(nb * TM, D_MODEL)
    r1 = lax.broadcasted_iota(jnp.int32, (1, hrows, 1), 1)
    keep = ((r1 >= FFN_HALO) | prev_ok) & ((r1 < FFN_HALO + TM) | next_ok)
    acc_ref[...] = jnp.zeros_like(acc_ref)
    for c0 in range(0, D_FF, FF_CHUNK):
        cs = slice(c0, c0 + FF_CHUNK)
        g = _dot(xb_all, wg_ref[:, cs]).reshape(nb, hrows, FF_CHUNK)
        g_ref[...] = jnp.where(keep, g, 0.0)
        gc = cb_ref[:, cs].reshape(1, 1, FF_CHUNK)
        for jj in range(3):
            gc = gc + cw_ref[jj:jj + 1, cs].reshape(1, 1, FF_CHUNK) * g_ref[:, pl.ds(FFN_HALO - 1 + jj, TM), :]
        up = _dot(xb, wu_ref[:, cs])
        act = _gelu_tanh(gc).reshape(nb * TM, FF_CHUNK) * up
        acc_ref[...] += _dot(act.astype(BF16), wo_ref[cs, :])
    hn2 = ALPHA * hm + m[:, 5:6, :] * acc_ref[...].reshape(nb, TM, D_MODEL)
    o_ref[...] = _layer_norm_rows(hn2, lg_ref[...], lb_ref[...])


def _ffn_call(h, mod, wg_b, wu_b, wo_b, conv_w, conv_b, ln_g, ln_b, nb, skip_ctx):
    b, l, d = h.shape
    nblk = l // TM
    first = 1 if skip_ctx else 0
    per = TM // FFN_HALO
    last = l // FFN_HALO - 1
    vec = lambda n: pl.BlockSpec((1, n), lambda i, t: (0, 0))
    res = lambda shape: pl.BlockSpec(shape, lambda i, t: (0, 0), pipeline_mode=pl.Buffered(1))
    return pl.pallas_call(
        functools.partial(_ffn_kernel, nb=nb, nblk=nblk, first_blk=first),
        grid=(b // nb, nblk - first),
        in_specs=[pl.BlockSpec((nb, TM, d), lambda i, t: (i, t + first, 0)),
                  pl.BlockSpec((nb, FFN_HALO, d), lambda i, t: (i, jnp.maximum((t + first) * per - 1, 0), 0)),
                  pl.BlockSpec((nb, FFN_HALO, d), lambda i, t: (i, jnp.minimum((t + first + 1) * per, last), 0)),
                  pl.BlockSpec((nb, 1, 6, d), lambda i, t: (i, jnp.minimum(t + first, 1), 0, 0)),
                  res((d, D_FF)), res((d, D_FF)), res((D_FF, d)),
                  pl.BlockSpec((3, D_FF), lambda i, t: (0, 0)), vec(D_FF), vec(d), vec(d)],
        out_specs=pl.BlockSpec((nb, TM, d), lambda i, t: (i, t, 0)),
        out_shape=jax.ShapeDtypeStruct((b, l - first * TM, d), F32),
        scratch_shapes=[pltpu.VMEM((nb * TM, d), F32),
                        pltpu.VMEM((nb, TM + 2 * FFN_HALO, FF_CHUNK), F32)],
        compiler_params=_cparams(("parallel", "parallel")),
        name="conv_ffn",
    )(h, h, h, mod, wg_b, wu_b, wo_b, conv_w, conv_b.reshape(1, D_FF), ln_g.reshape(1, d), ln_b.reshape(1, d))


def _rope_tables(n):
    rows = n // GRID_W
    row = jnp.repeat(jnp.arange(rows), GRID_W)
    col = jnp.tile(jnp.arange(GRID_W), rows)
    n_freq = D_HEAD // 4
    inv = ROPE_BASE ** (-jnp.arange(n_freq, dtype=F32) / n_freq)
    ang = jnp.concatenate([row[:, None] * inv, col[:, None] * inv], axis=-1)
    ang = jnp.concatenate([jnp.zeros((CTX_LEN, D_HEAD // 2), F32), ang], axis=0)
    cos, sin = jnp.cos(ang), jnp.sin(ang)
    return jnp.concatenate([cos, cos], axis=-1), jnp.concatenate([-sin, sin], axis=-1)


def kernel(x, c, ctx, c_ctx, ada_w, ada_b, w_in, hgrn_lb_logits, hgrn_norm_w, ssm_conv_w, ssm_conv_b, ssm_dt_bias, ssm_a_log, ssm_d, ssm_norm_w, lru_conv_w, lru_conv_b, lru_wa, lru_ba, lru_wi, lru_bi, lru_lambda, ret_decay_logit, w_out, ln1_g, ln1_b, ffn_w_in, ffn_conv_w, ffn_conv_b, ffn_w_out, ln2_g, ln2_b):
    bsz, n, d = x.shape
    assert d == D_MODEL and ctx.shape[1] == CTX_LEN == TM and n % TM == 0 and n % GRID_W == 0
    nb = 2 if bsz % 2 == 0 else 1
    depth = ada_w.shape[0]

    cosf, sinf = _rope_tables(n)
    p_soft = jax.nn.softmax(hgrn_lb_logits.astype(F32), axis=0)
    lb_all = jnp.cumsum(p_soft, axis=0) - p_soft

    rows = -(-(bsz + 1) // 8) * 8
    cond = jnp.zeros((rows, d), F32).at[:bsz].set(c).at[bsz].set(c_ctx)
    mods = _ada_call(cond, ada_w, ada_b)

    n_in = w_in.shape[2]
    dt0 = COL_BXBC + XBC_W
    h = jnp.concatenate([ctx, x], axis=1)
    for l in range(depth):
        ml = mods[l].reshape(rows, 6, d)
        mod = jnp.stack([jnp.broadcast_to(ml[bsz][None], (bsz, 6, d)), ml[:bsz]], axis=1)
        w_l = w_in[l]
        w_perm = jnp.concatenate([w_l[:, :dt0], w_l[:, dt0 + 2 * B_HEADS:], w_l[:, dt0:dt0 + 2 * B_HEADS],
                                  jnp.zeros((d, P_W - n_in), F32)], axis=1).astype(BF16)
        p = _inproj_call(h, mod, w_perm, nb)
        ya = _mixa_call(p, lb_all[l])
        yb = _mixb_call(p, ssm_conv_w[l], ssm_conv_b[l], ssm_dt_bias[l], ssm_a_log[l], ssm_d[l])
        yc = _mixc_call(p, lru_conv_w[l], lru_conv_b[l], lru_wa[l], lru_ba[l], lru_wi[l], lru_bi[l],
                        lru_lambda[l])
        yd = _mixd_call(p, cosf, sinf, ret_decay_logit[l].astype(F32))
        h = _outproj_call(ya, yb, yc, yd, p, h, mod, w_out[l].astype(BF16), hgrn_norm_w[l], ssm_norm_w[l],
                          ln1_g[l], ln1_b[l], nb)
        wf = ffn_w_in[l]
        h = _ffn_call(h, mod, wf[:, :D_FF].astype(BF16), wf[:, D_FF:].astype(BF16),
                      ffn_w_out[l].astype(BF16), ffn_conv_w[l], ffn_conv_b[l], ln2_g[l], ln2_b[l],
                      nb, skip_ctx=(l == depth - 1))
    return h
```

```python
import functools
import math

import jax
import jax.numpy as jnp
from jax import lax
from jax.experimental import pallas as pl
from jax.experimental.pallas import tpu as pltpu

F32 = jnp.float32
BF16 = jnp.bfloat16

D_MODEL = 1024
DEPTH = 4
GRID_W = 64
CTX_LEN = 256
GROUP_W = 512
A_HEAD = 128
A_HEADS = 4
LB_FLOOR = 1e-30
B_HEADDIM = 64
B_HEADS = 8
B_GROUPS = 2
B_STATE = 128
XBC_W = GROUP_W + 2 * B_GROUPS * B_STATE
C_BLOCKS = 8
C_BLOCK = 64
C_POW = 8.0
D_HEAD = 128
D_HEADS = 4
ROPE_BASE = 10000.0
D_FF = 2816
ALPHA = (2 * DEPTH) ** 0.25
EPS = 1e-6
LOG2E = 1.4426950408889634

TM = 256
FFN_HALO = 8
A_CHUNK = 64
A_SUB = 8
A_FBLK = 16
A_GUARD = 80.0
B_CHUNK = 128
D_CHUNK = 128
FF_CHUNK = 256
VMEM_LIMIT = 56 * 1024 * 1024

COL_AQ, COL_AI, COL_AF, COL_AG = 0, 512, 1024, 2048
COL_BZ, COL_BXBC = 2560, 3072
COL_CX, COL_CG = 4096, 4608
COL_DQ, COL_DK, COL_DV, COL_DG = 5120, 5632, 6144, 6656
COL_DT = 7168
P_W = 7296
DT_W = 128
CONV_LO, CONV_HI = COL_BXBC, COL_CG


def _cparams(sem):
    return pltpu.CompilerParams(dimension_semantics=sem, vmem_limit_bytes=VMEM_LIMIT)


def _softplus(x):
    return jnp.maximum(x, 0.0) + jnp.log1p(jnp.exp(-jnp.abs(x)))


def _sigmoid(x):
    return 1.0 / (1.0 + jnp.exp(-x))


def _silu(x):
    return x * _sigmoid(x)


def _gelu_tanh(x):
    return 0.5 * x * (1.0 + jnp.tanh(math.sqrt(2.0 / math.pi) * (x + 0.044715 * (x * x * x))))


def _dot(a, b):
    return jnp.dot(a, b, preferred_element_type=F32)


def _dot_nt(a, b):
    return lax.dot_general(a, b, (((1,), (1,)), ((), ())), preferred_element_type=F32)


def _split3(x):
    hi = x.astype(BF16)
    r1 = x - hi.astype(F32)
    mid = r1.astype(BF16)
    lo = (r1 - mid.astype(F32)).astype(BF16)
    return hi, mid, lo


def _sel_dot(m01, x):
    hi, mid, lo = _split3(x)
    return _dot(m01, hi) + _dot(m01, mid) + _dot(m01, lo)


def _seq_block(d, j, nblk):
    return jnp.where(d == 0, j, jnp.where(j == 0, 0, nblk - j))


def _ada_kernel(c_ref, w_ref, b_ref, o_ref):
    s = _silu(c_ref[...])
    o_ref[0] = jnp.dot(s, w_ref[0], preferred_element_type=F32,
                       precision=lax.Precision.HIGHEST) + b_ref[0]


def _ada_call(cond, ada_w, ada_b):
    depth, d, n6 = ada_w.shape
    rows = cond.shape[0]
    tn = 1536
    return pl.pallas_call(
        _ada_kernel,
        grid=(depth, n6 // tn),
        in_specs=[pl.BlockSpec((rows, d), lambda l, n: (0, 0)),
                  pl.BlockSpec((1, d, tn), lambda l, n: (l, 0, n)),
                  pl.BlockSpec((1, 1, tn), lambda l, n: (l, 0, n))],
        out_specs=pl.BlockSpec((1, rows, tn), lambda l, n: (l, 0, n)),
        out_shape=jax.ShapeDtypeStruct((depth, rows, n6), F32),
        compiler_params=_cparams(("parallel", "parallel")),
        name="ada_mod",
    )(cond, ada_w, ada_b.reshape(depth, 1, n6))


def _inproj_kernel(h_ref, hp_ref, hn_ref, mod_ref, w_ref, cw_ref, cb_ref, o_ref, g_ref, *, nb, tn, nblk):
    rb = pl.program_id(1)
    prev_ok = rb >= 2
    next_ok = (rb >= 1) & (rb <= nblk - 2)
    m = mod_ref[:, 0]
    sc, sh = 1.0 + m[:, 1:2, :], m[:, 0:1, :]
    hrows = TM + 2 * FFN_HALO
    ub = (h_ref[...] * sc + sh).astype(BF16).reshape(nb * TM, D_MODEL)
    for n0 in list(range(0, CONV_LO, tn)) + list(range(CONV_HI, P_W, tn)):
        w = min(tn, P_W - n0)
        o_ref[:, :, n0:n0 + w] = _dot(ub, w_ref[:, n0:n0 + w]).astype(BF16).reshape(nb, TM, w)
    u_all = (jnp.concatenate([hp_ref[...], h_ref[...], hn_ref[...]], axis=1) * sc + sh).astype(BF16)
    u_all = u_all.reshape(nb * hrows, D_MODEL)
    r1 = lax.broadcasted_iota(jnp.int32, (1, hrows, 1), 1)
    keep = ((r1 >= FFN_HALO) | prev_ok) & ((r1 < FFN_HALO + TM) | next_ok)
    for n0 in range(CONV_LO, CONV_HI, tn):
        cs = slice(n0 - CONV_LO, n0 - CONV_LO + tn)
        g = _dot(u_all, w_ref[:, n0:n0 + tn]).reshape(nb, hrows, tn)
        g_ref[...] = jnp.where(keep, g, 0.0)
        acc = cb_ref[:, cs].reshape(1, 1, tn)
        for jj in range(4):
            acc = acc + cw_ref[jj:jj + 1, cs].reshape(1, 1, tn) * g_ref[:, pl.ds(FFN_HALO - 2 + jj, TM), :]
        if n0 < COL_CX:
            acc = _silu(acc)
        o_ref[:, :, n0:n0 + tn] = acc.astype(BF16)


def _inproj_call(h, mod, w_in_b, conv_w, conv_b, nb):
    b, l, d = h.shape
    nblk = l // TM
    per = TM // FFN_HALO
    last = l // FFN_HALO - 1
    ncv = CONV_HI - CONV_LO
    kern = functools.partial(_inproj_kernel, nb=nb, tn=512, nblk=nblk)
    return pl.pallas_call(
        kern,
        grid=(b // nb, nblk),
        in_specs=[pl.BlockSpec((nb, TM, d), lambda i, t: (i, t, 0)),
                  pl.BlockSpec((nb, FFN_HALO, d), lambda i, t: (i, jnp.maximum(t * per - 1, 0), 0)),
                  pl.BlockSpec((nb, FFN_HALO, d), lambda i, t: (i, jnp.minimum((t + 1) * per, last), 0)),
                  pl.BlockSpec((nb, 1, 6, d), lambda i, t: (i, jnp.minimum(t, 1), 0, 0)),
                  pl.BlockSpec((d, P_W), lambda i, t: (0, 0), pipeline_mode=pl.Buffered(1)),
                  pl.BlockSpec((4, ncv), lambda i, t: (0, 0)),
                  pl.BlockSpec((1, ncv), lambda i, t: (0, 0))],
        out_specs=pl.BlockSpec((nb, TM, P_W), lambda i, t: (i, t, 0)),
        out_shape=jax.ShapeDtypeStruct((b, l, P_W), BF16),
        scratch_shapes=[pltpu.VMEM((nb, TM + 2 * FFN_HALO, 512), F32)],
        compiler_params=_cparams(("parallel", "parallel")),
        name="in_proj",
    )(h, h, h, mod, w_in_b, conv_w, conv_b)


def _mixa_body(q_ref, v_ref, f_ref, lb_ref, y_ref, st_ref, sh_ref, rev):
    C, m = A_CHUNK, A_SUB
    nsl = C // m - 1
    lbv = lb_ref[0]
    log_lb = jnp.log(jnp.maximum(lbv, LB_FLOOR))
    log_1mlb = jnp.log1p(-lbv)
    one_m_lb = 1.0 - lbv

    row = lax.broadcasted_iota(jnp.int32, (C, C), 0)
    col = lax.broadcasted_iota(jnp.int32, (C, C), 1)
    tau_r = (C - 1 - row) if rev else row
    tau_c = (C - 1 - col) if rev else col
    blk_r, blk_c = tau_r // m, tau_c // m
    tri = (tau_c <= tau_r).astype(BF16)
    refrow = (C - m * blk_r) if rev else (m * blk_r - 1)
    rsel = ((col == refrow) & (blk_r >= 1)).astype(BF16)
    off_mask = blk_c < blk_r
    r1 = lax.broadcasted_iota(jnp.int32, (C, 1), 0)
    tau1 = (C - 1 - r1) if rev else r1
    blk1 = tau1 // m
    off1 = tau1 % m

    pad = m
    for c in (range(TM // C - 1, -1, -1) if rev else range(TM // C)):
        sl = pl.ds(c * C, C)
        q = q_ref[0, sl, :].astype(F32)
        v = v_ref[0, sl, :].astype(F32)
        u = f_ref[0, sl, :].astype(F32)
        log_sig = jnp.minimum(u, 0.0) - jnp.log1p(jnp.exp(-jnp.abs(u)))
        t2 = log_1mlb + log_sig
        log_f = jnp.maximum(log_lb, t2) + jnp.log1p(jnp.exp(-jnp.abs(log_lb - t2)))
        k = one_m_lb * _sigmoid(-u)
        b = _sel_dot(tri, log_f)
        rr = _sel_dot(rsel, b)
        qt = q * jnp.exp(jnp.minimum(b - rr, 0.0))
        sh_ref[0, pl.ds(pad, C), :] = k
        sh_ref[1, pl.ds(pad, C), :] = b
        sh_ref[2, pl.ds(pad, C), :] = v
        y_parts = []
        for h in range(A_HEADS):
            hs = slice(h * A_HEAD, (h + 1) * A_HEAD)
            qh, kh, vh, bh, qth = q[:, hs], k[:, hs], v[:, hs], b[:, hs], qt[:, hs]
            tot = bh[0:1, :] if rev else bh[C - 1:C, :]
            q_aug = jnp.concatenate(
                [jnp.where(blk1 == i, qth, 0.0) for i in range(1, nsl + 1)], axis=1).astype(BF16)
            k_parts = []
            for i in range(1, nsl + 1):
                rrow = (C - m * i) if rev else (m * i - 1)
                k_parts.append(kh * jnp.exp(jnp.minimum(bh[rrow:rrow + 1, :] - bh, 0.0)))
            k_aug = jnp.concatenate(k_parts, axis=1).astype(BF16)
            p_off = jnp.where(off_mask, _dot_nt(q_aug, k_aug), 0.0)
            y = _dot(p_off.astype(BF16), vh.astype(BF16))
            for dl in range(m):
                if dl == 0:
                    ks, bs, vs = kh, bh, vh
                else:
                    st = pad + dl if rev else pad - dl
                    ks = sh_ref[0, pl.ds(st, C), hs]
                    bs = sh_ref[1, pl.ds(st, C), hs]
                    vs = sh_ref[2, pl.ds(st, C), hs]
                tmp = qh * ks * jnp.exp(jnp.minimum(bh - bs, 0.0))
                p = jnp.sum(tmp, axis=-1, keepdims=True)
                y = y + jnp.where(off1 >= dl, p, 0.0) * vs
            st_t = st_ref[h]
            y = y + _dot_nt((qh * jnp.exp(bh)).astype(BF16), st_t.astype(BF16))
            kdec = (kh * jnp.exp(tot - bh)).astype(BF16)
            st_ref[h] = st_t * jnp.exp(tot) + _dot(vh.T.astype(BF16), kdec)
            y_parts.append(y)
        y_ref[0, 0, sl, :] = jnp.concatenate(y_parts, axis=1).astype(BF16)


def _mixa_tables(tm_ref, code_ref, rev):
    C, m = A_CHUNK, A_FBLK
    row = lax.broadcasted_iota(jnp.int32, (TM, TM), 0)
    col = lax.broadcasted_iota(jnp.int32, (TM, TM), 1)
    same_chunk = (row // C) == (col // C)
    tr, tc = row % C, col % C
    if rev:
        tr, tc = C - 1 - tr, C - 1 - tc
    same_blk = same_chunk & ((tr // m) == (tc // m))
    orr, oc = tr % m, tc % m
    half = m // 2
    tm_ref[0] = (same_chunk & (tc <= tr)).astype(BF16)
    tm_ref[1] = (same_blk & (oc <= orr)).astype(BF16)
    plus = same_blk & (oc >= half) & (oc <= orr)
    minus = same_blk & (oc > orr) & (oc < half)
    tm_ref[2] = (plus.astype(F32) - minus.astype(F32)).astype(BF16)
    off = same_chunk & ((tc // m) < (tr // m))
    diag = same_blk & (tc <= tr)
    code_ref[...] = jnp.where(off, 1.0, jnp.where(diag, 2.0, 0.0))


def _mixa_fast(q_ref, v_ref, y_ref, st_ref, code_ref, k_ref, b_ref, bl_ref, bm_ref, rev):
    C, m = A_CHUNK, A_FBLK
    nch = TM // C
    nsl = C // m - 1
    code = code_ref[...]
    off_mask = code == 1.0
    diag_mask = code == 2.0
    r1 = lax.broadcasted_iota(jnp.int32, (TM, 1), 0) % C
    blk1 = ((C - 1 - r1) if rev else r1) // m
    rows = lambda a, c: a[c * C:(c + 1) * C, :]
    order = range(nch - 1, -1, -1) if rev else range(nch)
    y_parts = []
    for h in range(A_HEADS):
        hs = slice(h * A_HEAD, (h + 1) * A_HEAD)
        qh = q_ref[0, :, hs].astype(F32)
        vb = v_ref[0, :, hs]
        kh, bh, blh, bmh = k_ref[:, hs], b_ref[:, hs], bl_ref[:, hs], bm_ref[:, hs]
        qt = qh * jnp.exp2(blh)
        q_aug = jnp.concatenate(
            [jnp.where(blk1 == i, qt, 0.0) for i in range(1, nsl + 1)], axis=1).astype(BF16)
        k_parts = []
        for i in range(1, nsl + 1):
            rrow = (C - m * i) if rev else (m * i - 1)
            dif = jnp.concatenate([bh[c * C + rrow:c * C + rrow + 1, :] - rows(bh, c) for c in range(nch)], axis=0)
            k_parts.append(kh * jnp.exp2(jnp.minimum(dif, 0.0)))
        k_aug = jnp.concatenate(k_parts, axis=1).astype(BF16)
        p = jnp.where(off_mask, _dot_nt(q_aug, k_aug), 0.0)
        qd = (qh * jnp.exp2(bmh)).astype(BF16)
        kd = (kh * jnp.exp2(-bmh)).astype(BF16)
        p = p + jnp.where(diag_mask, _dot_nt(qd, kd), 0.0)
        y = _dot(p.astype(BF16), vb)
        qe = (qh * jnp.exp2(bh)).astype(BF16)
        tots = [bh[c * C:c * C + 1, :] if rev else bh[(c + 1) * C - 1:(c + 1) * C, :] for c in range(nch)]
        upd = [_dot(rows(vb, c).astype(F32).T.astype(BF16),
                    (rows(kh, c) * jnp.exp2(tots[c] - rows(bh, c))).astype(BF16)) for c in range(nch)]
        st = st_ref[h]
        y_inter = [None] * nch
        for c in order:
            y_inter[c] = _dot_nt(rows(qe, c), st.astype(BF16))
            st = st * jnp.exp2(tots[c]) + upd[c]
        st_ref[h] = st
        y_parts.append(y + jnp.concatenate(y_inter, axis=0))
    y_ref[0, 0] = jnp.concatenate(y_parts, axis=1).astype(BF16)


def _mixa_dir(q_ref, v_ref, f_ref, lb_ref, y_ref, st_ref, sh_ref, tm_ref, code_ref, k_ref, b_ref, bl_ref, bm_ref,
              rev):
    @pl.when(pl.program_id(2) == 0)
    def _():
        _mixa_tables(tm_ref, code_ref, rev)

    lbv = lb_ref[0]
    log_lb = jnp.log(jnp.maximum(lbv, LB_FLOOR))
    log_1mlb = jnp.log1p(-lbv)
    one_m_lb = 1.0 - lbv
    worst = jnp.zeros((1, 1), F32)
    lw = 2 * A_HEAD
    for h in range(GROUP_W // lw):
        hs = slice(h * lw, (h + 1) * lw)
        u = f_ref[0, :, hs].astype(F32)
        log_sig = jnp.minimum(u, 0.0) - jnp.log(1.0 + jnp.exp(-jnp.abs(u)))
        t2 = log_1mlb[:, hs] + log_sig
        log_f = jnp.maximum(log_lb[:, hs], t2) + jnp.log(1.0 + jnp.exp(-jnp.abs(log_lb[:, hs] - t2)))
        k_ref[:, hs] = one_m_lb[:, hs] * jnp.exp(log_sig - u)
        lf2 = log_f * LOG2E
        hi = lf2.astype(BF16)
        lo = (lf2 - hi.astype(F32)).astype(BF16)
        cum = lambda i: _dot(tm_ref[i], hi) + _dot(tm_ref[i], lo)
        b_ref[:, hs] = cum(0)
        bl_ref[:, hs] = cum(1)
        bm = cum(2)
        bm_ref[:, hs] = bm
        worst = jnp.maximum(worst, jnp.max(jnp.abs(bm), keepdims=True))
    ok = worst[0, 0] <= A_GUARD * LOG2E

    @pl.when(ok)
    def _():
        _mixa_fast(q_ref, v_ref, y_ref, st_ref, code_ref, k_ref, b_ref, bl_ref, bm_ref, rev)

    @pl.when(jnp.logical_not(ok))
    def _():
        _mixa_body(q_ref, v_ref, f_ref, lb_ref, y_ref, st_ref, sh_ref, rev)


def _mixa_kernel(q_ref, v_ref, f_ref, lb_ref, y_ref, st_ref, sh_ref, tm_ref, code_ref, k_ref, b_ref, bl_ref,
                 bm_ref):
    d = pl.program_id(1)
    j = pl.program_id(2)

    @pl.when(j == 0)
    def _():
        st_ref[...] = jnp.zeros_like(st_ref)
        sh_ref[...] = jnp.zeros_like(sh_ref)

    @pl.when(d == 0)
    def _():
        _mixa_dir(q_ref, v_ref, f_ref, lb_ref, y_ref, st_ref, sh_ref, tm_ref, code_ref, k_ref, b_ref, bl_ref,
                  bm_ref, False)

    @pl.when(d == 1)
    def _():
        _mixa_dir(q_ref, v_ref, f_ref, lb_ref, y_ref, st_ref, sh_ref, tm_ref, code_ref, k_ref, b_ref, bl_ref,
                  bm_ref, True)


def _mixa_call(p, lb):
    b, l, _ = p.shape
    nblk = l // TM
    cb = lambda col: col // GROUP_W
    rowmap = lambda col: (lambda i, d, j: (i, _seq_block(d, j, nblk), cb(col)))
    return pl.pallas_call(
        _mixa_kernel,
        grid=(b, 2, nblk),
        in_specs=[pl.BlockSpec((1, TM, GROUP_W), rowmap(COL_AQ)),
                  pl.BlockSpec((1, TM, GROUP_W), rowmap(COL_AI)),
                  pl.BlockSpec((1, TM, GROUP_W), lambda i, d, j: (i, _seq_block(d, j, nblk), cb(COL_AF) + d)),
                  pl.BlockSpec((1, 1, GROUP_W), lambda i, d, j: (d, 0, 0))],
        out_specs=pl.BlockSpec((1, 1, TM, GROUP_W), lambda i, d, j: (d, i, _seq_block(d, j, nblk), 0)),
        out_shape=jax.ShapeDtypeStruct((2, b, l, GROUP_W), BF16),
        scratch_shapes=[pltpu.VMEM((A_HEADS, A_HEAD, A_HEAD), F32),
                        pltpu.VMEM((3, A_CHUNK + 2 * A_SUB, GROUP_W), F32),
                        pltpu.VMEM((3, TM, TM), BF16),
                        pltpu.VMEM((TM, TM), F32),
                        pltpu.VMEM((TM, GROUP_W), F32),
                        pltpu.VMEM((TM, GROUP_W), F32),
                        pltpu.VMEM((TM, GROUP_W), F32),
                        pltpu.VMEM((TM, GROUP_W), F32)],
        compiler_params=_cparams(("parallel", "parallel", "arbitrary")),
        name="mix_hgrn2",
    )(p, p, p, lb.reshape(2, 1, GROUP_W))


def _mixb_body(xc_ref, dt_ref, dtb_ref, alog_ref, e_ref, dsk_ref, y_ref, st_ref, rev, d):
    C = B_CHUNK
    hpg = B_HEADS // B_GROUPS
    gw = hpg * B_HEADDIM
    row = lax.broadcasted_iota(jnp.int32, (C, C), 0)
    col = lax.broadcasted_iota(jnp.int32, (C, C), 1)
    trimask = (col >= row) if rev else (col <= row)
    tri = trimask.astype(BF16)
    lane_head = lax.broadcasted_iota(jnp.int32, (1, gw), 1) // B_HEADDIM
    e01 = e_ref[0]
    dtb = dtb_ref[0]
    neg_a = -jnp.exp(alog_ref[0])

    def expand(x):
        hi, mid, lo = _split3(x)
        return _dot(hi, e01) + _dot(mid, e01) + _dot(lo, e01)

    for c in (range(TM // C - 1, -1, -1) if rev else range(TM // C)):
        sl = pl.ds(c * C, C)
        bx = xc_ref[0, sl, 0:GROUP_W].astype(F32)
        bb = xc_ref[0, sl, GROUP_W:GROUP_W + B_GROUPS * B_STATE]
        bc = xc_ref[0, sl, GROUP_W + B_GROUPS * B_STATE:XBC_W]
        dt = _softplus(dt_ref[0, sl, :].astype(F32) + dtb)
        la = dt * neg_a
        b = _sel_dot(tri, la)
        b_exp = expand(b)
        dt_exp = expand(dt)
        tot_exp = b_exp[0:1, :] if rev else b_exp[C - 1:C, :]
        v_all = bx * dt_exp
        vdec = v_all * jnp.exp(tot_exp - b_exp)
        eb = jnp.exp(b_exp)
        bt = b.T
        y_parts = []
        for g in range(B_GROUPS):
            gs = slice(g * gw, (g + 1) * gw)
            bg = bb[:, g * B_STATE:(g + 1) * B_STATE]
            cg = bc[:, g * B_STATE:(g + 1) * B_STATE]
            qk = _dot_nt(cg, bg)
            s_g = st_ref[g]
            vg = v_all[:, gs].astype(BF16)
            m_parts, v_parts = [], []
            for e in range(hpg):
                lane = 8 * d + g * hpg + e
                rel = jnp.where(trimask,
                                jnp.exp(jnp.minimum(b[:, lane:lane + 1] - bt[lane:lane + 1, :], 0.0)), 0.0)
                m_parts.append((qk * rel).astype(BF16))
                v_parts.append(jnp.where(lane_head == e, vg, jnp.zeros_like(vg)))
            y_g = (_dot(jnp.concatenate(m_parts, axis=1), jnp.concatenate(v_parts, axis=0))
                   + _dot(cg, s_g.astype(BF16)) * eb[:, gs])
            st_ref[g] = (s_g * jnp.exp(tot_exp[:, gs])
                         + _dot(bg.astype(F32).T.astype(BF16), vdec[:, gs].astype(BF16)))
            y_parts.append(y_g)
        y = jnp.concatenate(y_parts, axis=1)
        if not rev:
            y = y + dsk_ref[...] * bx
        y_ref[0, 0, sl, :] = y.astype(BF16)


def _mixb_kernel(x_ref, dt_ref, dtb_ref, alog_ref, e_ref, dsk_ref, y_ref, st_ref):
    d = pl.program_id(1)
    j = pl.program_id(2)

    @pl.when(j == 0)
    def _():
        st_ref[...] = jnp.zeros_like(st_ref)

    @pl.when(d == 0)
    def _():
        _mixb_body(x_ref, dt_ref, dtb_ref, alog_ref, e_ref, dsk_ref, y_ref, st_ref, False, 0)

    @pl.when(d == 1)
    def _():
        _mixb_body(x_ref, dt_ref, dtb_ref, alog_ref, e_ref, dsk_ref, y_ref, st_ref, True, 1)


def _mixb_call(p, dt_bias, a_log, ssm_d):
    b, l, _ = p.shape
    nblk = l // TM
    pad_dir = lambda a: jnp.stack([jnp.pad(a[0], (0, DT_W - B_HEADS)),
                                   jnp.pad(a[1], (B_HEADS, DT_W - 2 * B_HEADS))]).reshape(2, 1, DT_W)
    lanes = jnp.arange(GROUP_W) // B_HEADDIM
    e01 = jnp.stack([(jnp.arange(DT_W)[:, None] == lanes[None, :] + 8 * dd) for dd in range(2)]).astype(BF16)
    dsk = jnp.repeat(ssm_d, B_HEADDIM).reshape(1, GROUP_W)
    xcol = COL_BXBC // XBC_W
    return pl.pallas_call(
        _mixb_kernel,
        grid=(b, 2, nblk),
        in_specs=[pl.BlockSpec((1, TM, XBC_W), lambda i, d, j: (i, _seq_block(d, j, nblk), xcol)),
                  pl.BlockSpec((1, TM, DT_W), lambda i, d, j: (i, _seq_block(d, j, nblk), COL_DT // DT_W)),
                  pl.BlockSpec((1, 1, DT_W), lambda i, d, j: (d, 0, 0)),
                  pl.BlockSpec((1, 1, DT_W), lambda i, d, j: (d, 0, 0)),
                  pl.BlockSpec((1, DT_W, GROUP_W), lambda i, d, j: (d, 0, 0)),
                  pl.BlockSpec((1, GROUP_W), lambda i, d, j: (0, 0))],
        out_specs=pl.BlockSpec((1, 1, TM, GROUP_W), lambda i, d, j: (d, i, _seq_block(d, j, nblk), 0)),
        out_shape=jax.ShapeDtypeStruct((2, b, l, GROUP_W), BF16),
        scratch_shapes=[pltpu.VMEM((B_GROUPS, B_STATE, GROUP_W // B_GROUPS), F32)],
        compiler_params=_cparams(("parallel", "parallel", "arbitrary")),
        name="mix_ssd",
    )(p, p, pad_dir(dt_bias), pad_dir(a_log), e01, dsk)


def _mixc_body(y_ref, carry_ref, a_ref, b_ref, rev):
    grp = TM // 8
    a3 = a_ref[...].reshape(grp, 8, GROUP_W)
    b3 = b_ref[...].reshape(grp, 8, GROUP_W)
    ro = lax.broadcasted_iota(jnp.int32, (1, 8, 1), 1)
    for s in (1, 2, 4):
        valid = (ro + s <= 7) if rev else (ro >= s)
        shift = (8 - s) if rev else s
        a_sh = pltpu.roll(a3, shift, 1)
        b_sh = pltpu.roll(b3, shift, 1)
        b3 = b3 + a3 * jnp.where(valid, b_sh, 0.0)
        a3 = a3 * jnp.where(valid, a_sh, 1.0)
    cb = jnp.broadcast_to(carry_ref[...], (8, GROUP_W))
    last = 0 if rev else 7
    hs = [None] * grp
    for g in (range(grp - 1, -1, -1) if rev else range(grp)):
        hg = a3[g] * cb + b3[g]
        hs[g] = hg
        cb = jnp.broadcast_to(hg[last:last + 1, :], (8, GROUP_W))
    carry_ref[...] = cb[0:1, :]
    y_ref[0, 0] = jnp.concatenate(hs, axis=0).astype(BF16)


def _mixc_kernel(x_ref, wa_ref, wi_ref, ba_ref, bi_ref, lam_ref, y_ref, carry_ref, a_ref, b_ref):
    d = pl.program_id(1)
    j = pl.program_id(2)

    @pl.when(j == 0)
    def _():
        carry_ref[...] = jnp.zeros_like(carry_ref)

    cxb = x_ref[0]
    cx = cxb.astype(F32)
    r = _sigmoid(_dot(cxb, wa_ref[0]) + ba_ref[0])
    gi = _sigmoid(_dot(cxb, wi_ref[0]) + bi_ref[0])
    la = (-C_POW) * r * _softplus(-lam_ref[0])
    a_ref[...] = jnp.exp(la)
    b_ref[...] = jnp.sqrt(1.0 - jnp.exp(2.0 * la)) * gi * cx

    @pl.when(d == 0)
    def _():
        _mixc_body(y_ref, carry_ref, a_ref, b_ref, False)

    @pl.when(d == 1)
    def _():
        _mixc_body(y_ref, carry_ref, a_ref, b_ref, True)


def _block_diag(w):
    nd, nb_, n, _ = w.shape
    eye = jnp.eye(nb_, dtype=w.dtype)
    return jnp.einsum("dhij,hg->dhigj", w, eye).reshape(nd, nb_ * n, nb_ * n)


def _mixc_call(p, wa, ba, wi, bi, lam):
    b, l, _ = p.shape
    nblk = l // TM
    xcol = COL_CX // GROUP_W
    vec = lambda a: a.reshape(2, 1, GROUP_W)
    dspec = pl.BlockSpec((1, 1, GROUP_W), lambda i, d, j: (d, 0, 0))
    wspec = pl.BlockSpec((1, GROUP_W, GROUP_W), lambda i, d, j: (d, 0, 0))
    return pl.pallas_call(
        _mixc_kernel,
        grid=(b, 2, nblk),
        in_specs=[pl.BlockSpec((1, TM, GROUP_W), lambda i, d, j: (i, _seq_block(d, j, nblk), xcol)),
                  wspec, wspec, dspec, dspec, dspec],
        out_specs=pl.BlockSpec((1, 1, TM, GROUP_W), lambda i, d, j: (d, i, _seq_block(d, j, nblk), 0)),
        out_shape=jax.ShapeDtypeStruct((2, b, l, GROUP_W), BF16),
        scratch_shapes=[pltpu.VMEM((1, GROUP_W), F32),
                        pltpu.VMEM((TM, GROUP_W), F32),
                        pltpu.VMEM((TM, GROUP_W), F32)],
        compiler_params=_cparams(("parallel", "parallel", "arbitrary")),
        name="mix_rglru",
    )(p, _block_diag(wa).astype(BF16),
      _block_diag(wi).astype(BF16), vec(ba), vec(bi), vec(lam))


def _mixd_body(q_ref, k_ref, v_ref, cos_ref, sin_ref, lgt_ref, y_ref, st_ref, rev):
    C = D_CHUNK
    lg = -_softplus(-lgt_ref[0])
    row = lax.broadcasted_iota(jnp.int32, (C, C), 0)
    col = lax.broadcasted_iota(jnp.int32, (C, C), 1)
    trimask = (col >= row) if rev else (col <= row)
    dist = jnp.maximum((col - row) if rev else (row - col), 0).astype(F32)
    r1 = lax.broadcasted_iota(jnp.int32, (C, 1), 0)
    tau1 = ((C - 1 - r1) if rev else r1).astype(F32)
    scale = D_HEAD ** -0.5
    for c in (range(TM // C - 1, -1, -1) if rev else range(TM // C)):
        sl = pl.ds(c * C, C)
        cosf = cos_ref[sl, :]
        sinf = sin_ref[sl, :]
        y_parts = []
        for h in range(D_HEADS):
            hs = slice(h * D_HEAD, (h + 1) * D_HEAD)
            lgh = lg[h:h + 1, 0:1]
            qh = q_ref[0, sl, hs].astype(F32)
            kh = k_ref[0, sl, hs].astype(F32)
            vh = v_ref[0, sl, hs]
            qr = qh * cosf + pltpu.roll(qh, D_HEAD // 2, 1) * sinf
            kr = (kh * cosf + pltpu.roll(kh, D_HEAD // 2, 1) * sinf) * scale
            qb = qr.astype(BF16)
            qk = _dot_nt(qb, kr.astype(BF16))
            rel = jnp.where(trimask, jnp.exp(lgh * dist), 0.0)
            s_h = st_ref[h]
            y = _dot((qk * rel).astype(BF16), vh) + _dot(qb, s_h.astype(BF16)) * jnp.exp(lgh * (tau1 + 1.0))
            kdec = kr * jnp.exp(lgh * (C - 1.0 - tau1))
            st_ref[h] = s_h * jnp.exp(lgh * float(C)) + _dot(kdec.T.astype(BF16), vh)
            y_parts.append(y)
        y_ref[0, 0, sl, :] = jnp.concatenate(y_parts, axis=1).astype(BF16)


def _mixd_kernel(q_ref, k_ref, v_ref, cos_ref, sin_ref, lgt_ref, y_ref, st_ref):
    d = pl.program_id(1)
    j = pl.program_id(2)

    @pl.when(j == 0)
    def _():
        st_ref[...] = jnp.zeros_like(st_ref)

    @pl.when(d == 0)
    def _():
        _mixd_body(q_ref, k_ref, v_ref, cos_ref, sin_ref, lgt_ref, y_ref, st_ref, False)

    @pl.when(d == 1)
    def _():
        _mixd_body(q_ref, k_ref, v_ref, cos_ref, sin_ref, lgt_ref, y_ref, st_ref, True)


def _mixd_call(p, cosf, sinf, decay_logit):
    b, l, _ = p.shape
    nblk = l // TM
    cb = lambda col: col // GROUP_W
    rowmap = lambda col: (lambda i, d, j: (i, _seq_block(d, j, nblk), cb(col)))
    lgt = jnp.broadcast_to(jnp.pad(decay_logit, ((0, 0), (0, 8 - D_HEADS)))[:, :, None], (2, 8, D_HEAD))
    tab = pl.BlockSpec((TM, D_HEAD), lambda i, d, j: (_seq_block(d, j, nblk), 0))
    return pl.pallas_call(
        _mixd_kernel,
        grid=(b, 2, nblk),
        in_specs=[pl.BlockSpec((1, TM, GROUP_W), rowmap(COL_DQ)),
                  pl.BlockSpec((1, TM, GROUP_W), rowmap(COL_DK)),
                  pl.BlockSpec((1, TM, GROUP_W), rowmap(COL_DV)),
                  tab, tab,
                  pl.BlockSpec((1, 8, D_HEAD), lambda i, d, j: (d, 0, 0))],
        out_specs=pl.BlockSpec((1, 1, TM, GROUP_W), lambda i, d, j: (d, i, _seq_block(d, j, nblk), 0)),
        out_shape=jax.ShapeDtypeStruct((2, b, l, GROUP_W), BF16),
        scratch_shapes=[pltpu.VMEM((D_HEADS, D_HEAD, D_HEAD), F32)],
        compiler_params=_cparams(("parallel", "parallel", "arbitrary")),
        name="mix_retention",
    )(p, p, p, cosf, sinf, lgt)


def _layer_norm_rows(x, g, b):
    mu = jnp.mean(x, axis=-1, keepdims=True)
    xc = x - mu
    var = jnp.mean(xc * xc, axis=-1, keepdims=True)
    return xc * lax.rsqrt(var + EPS) * g + b


def _outproj_kernel(yaf, yab, ybf, ybb, ycf, ycb, ydf, ydb, ag_ref, bz_ref, cg_ref, dg_ref,
                    h_ref, mod_ref, w_ref, nwa_ref, nwb_ref, lg_ref, lb_ref, o_ref, *, nb):
    rows = nb * TM
    ld = lambda f, bk: (f[0].astype(F32) + bk[0].astype(F32)).reshape(rows, GROUP_W)
    gate = lambda r: r[...].astype(F32).reshape(rows, GROUP_W)
    ya = ld(yaf, yab)
    parts = []
    for h in range(A_HEADS):
        x = ya[:, h * A_HEAD:(h + 1) * A_HEAD]
        parts.append(x * lax.rsqrt(jnp.mean(x * x, axis=-1, keepdims=True) + EPS))
    oa = jnp.concatenate(parts, axis=1) * nwa_ref[...] * _silu(gate(ag_ref))
    yb = ld(ybf, ybb) * _silu(gate(bz_ref))
    gw = GROUP_W // B_GROUPS
    parts = []
    for g in range(B_GROUPS):
        x = yb[:, g * gw:(g + 1) * gw]
        parts.append(x * lax.rsqrt(jnp.mean(x * x, axis=-1, keepdims=True) + EPS))
    ob = jnp.concatenate(parts, axis=1) * nwb_ref[...]
    oc = ld(ycf, ycb) * _gelu_tanh(gate(cg_ref))
    yd = ld(ydf, ydb)
    parts = []
    for h in range(D_HEADS):
        x = yd[:, h * D_HEAD:(h + 1) * D_HEAD]
        mu = jnp.mean(x, axis=-1, keepdims=True)
        xc = x - mu
        parts.append(xc * lax.rsqrt(jnp.mean(xc * xc, axis=-1, keepdims=True) + EPS))
    od = jnp.concatenate(parts, axis=1) * _silu(gate(dg_ref))
    mix = (_dot(oa.astype(BF16), w_ref[0:GROUP_W, :])
           + _dot(ob.astype(BF16), w_ref[GROUP_W:2 * GROUP_W, :])
           + _dot(oc.astype(BF16), w_ref[2 * GROUP_W:3 * GROUP_W, :])
           + _dot(od.astype(BF16), w_ref[3 * GROUP_W:4 * GROUP_W, :]))
    m = mod_ref[:, 0]
    hn = ALPHA * h_ref[...] + m[:, 2:3, :] * mix.reshape(nb, TM, D_MODEL)
    o_ref[...] = _layer_norm_rows(hn, lg_ref[...], lb_ref[...])


def _outproj_call(ya, yb, yc, yd, p, h, mod, w_out_b, nwa, nwb, ln_g, ln_b, nb):
    b, l, d = h.shape
    ydir = lambda dd: pl.BlockSpec((1, nb, TM, GROUP_W), lambda i, t: (dd, i, t, 0))
    pcol = lambda col: pl.BlockSpec((nb, TM, GROUP_W), lambda i, t: (i, t, col // GROUP_W))
    vec = lambda n: pl.BlockSpec((1, n), lambda i, t: (0, 0))
    return pl.pallas_call(
        functools.partial(_outproj_kernel, nb=nb),
        grid=(b // nb, l // TM),
        in_specs=[ydir(0), ydir(1), ydir(0), ydir(1), ydir(0), ydir(1), ydir(0), ydir(1),
                  pcol(COL_AG), pcol(COL_BZ), pcol(COL_CG), pcol(COL_DG),
                  pl.BlockSpec((nb, TM, d), lambda i, t: (i, t, 0)),
                  pl.BlockSpec((nb, 1, 6, d), lambda i, t: (i, jnp.minimum(t, 1), 0, 0)),
                  pl.BlockSpec((4 * GROUP_W, d), lambda i, t: (0, 0), pipeline_mode=pl.Buffered(1)),
                  vec(GROUP_W), vec(GROUP_W), vec(d), vec(d)],
        out_specs=pl.BlockSpec((nb, TM, d), lambda i, t: (i, t, 0)),
        out_shape=jax.ShapeDtypeStruct((b, l, d), F32),
        compiler_params=_cparams(("parallel", "parallel")),
        name="out_proj",
    )(ya, ya, yb, yb, yc, yc, yd, yd, p, p, p, p, h, mod, w_out_b,
      nwa.reshape(1, GROUP_W), nwb.reshape(1, GROUP_W), ln_g.reshape(1, d), ln_b.reshape(1, d))


def _ffn_kernel(h_ref, hp_ref, hn_ref, mod_ref, wg_ref, wu_ref, wo_ref, cw_ref, cb_ref, lg_ref, lb_ref,
                o_ref, act_ref, g_ref, *, nb, nblk, first_blk):
    rb = pl.program_id(1) + first_blk
    prev_ok = rb >= 2
    next_ok = (rb >= 1) & (rb <= nblk - 2)
    m = mod_ref[:, 0]
    sh, sc = m[:, 3:4, :], m[:, 4:5, :]
    hm = h_ref[...]
    hrows = TM + 2 * FFN_HALO
    x_all = jnp.concatenate([hp_ref[...], hm, hn_ref[...]], axis=1) * (1.0 + sc) + sh
    xb_all = x_all.astype(BF16).reshape(nb * hrows, D_MODEL)
    xb = x_all[:, FFN_HALO:FFN_HALO + TM, :].astype(BF16).reshape(nb * TM, D_MODEL)
    r1 = lax.broadcasted_iota(jnp.int32, (1, hrows, 1), 1)
    keep = ((r1 >= FFN_HALO) | prev_ok) & ((r1 < FFN_HALO + TM) | next_ok)
    for c0 in range(0, D_FF, FF_CHUNK):
        cs = slice(c0, c0 + FF_CHUNK)
        g = _dot(xb_all, wg_ref[:, cs]).reshape(nb, hrows, FF_CHUNK)
        g_ref[...] = jnp.where(keep, g, 0.0)
        gc = cb_ref[:, cs].reshape(1, 1, FF_CHUNK)
        for jj in range(3):
            gc = gc + cw_ref[jj:jj + 1, cs].reshape(1, 1, FF_CHUNK) * g_ref[:, pl.ds(FFN_HALO - 1 + jj, TM), :]
        up = _dot(xb, wu_ref[:, cs])
        act_ref[:, cs] = (_gelu_tanh(gc).reshape(nb * TM, FF_CHUNK) * up).astype(BF16)
    f = _dot(act_ref[...], wo_ref[...])
    hn2 = ALPHA * hm + m[:, 5:6, :] * f.reshape(nb, TM, D_MODEL)
    o_ref[...] = _layer_norm_rows(hn2, lg_ref[...], lb_ref[...])


def _ffn_call(h, mod, wg_b, wu_b, wo_b, conv_w, conv_b, ln_g, ln_b, nb, skip_ctx):
    b, l, d = h.shape
    nblk = l // TM
    first = 1 if skip_ctx else 0
    per = TM // FFN_HALO
    last = l // FFN_HALO - 1
    vec = lambda n: pl.BlockSpec((1, n), lambda i, t: (0, 0))
    res = lambda shape: pl.BlockSpec(shape, lambda i, t: (0, 0), pipeline_mode=pl.Buffered(1))
    return pl.pallas_call(
        functools.partial(_ffn_kernel, nb=nb, nblk=nblk, first_blk=first),
        grid=(b // nb, nblk - first),
        in_specs=[pl.BlockSpec((nb, TM, d), lambda i, t: (i, t + first, 0)),
                  pl.BlockSpec((nb, FFN_HALO, d), lambda i, t: (i, jnp.maximum((t + first) * per - 1, 0), 0)),
                  pl.BlockSpec((nb, FFN_HALO, d), lambda i, t: (i, jnp.minimum((t + first + 1) * per, last), 0)),
                  pl.BlockSpec((nb, 1, 6, d), lambda i, t: (i, jnp.minimum(t + first, 1), 0, 0)),
                  res((d, D_FF)), res((d, D_FF)), res((D_FF, d)),
                  pl.BlockSpec((3, D_FF), lambda i, t: (0, 0)), vec(D_FF), vec(d), vec(d)],
        out_specs=pl.BlockSpec((nb, TM, d), lambda i, t: (i, t, 0)),
        out_shape=jax.ShapeDtypeStruct((b, l - first * TM, d), F32),
        scratch_shapes=[pltpu.VMEM((nb * TM, D_FF), BF16),
                        pltpu.VMEM((nb, TM + 2 * FFN_HALO, FF_CHUNK), F32)],
        compiler_params=_cparams(("parallel", "parallel")),
        name="conv_ffn",
    )(h, h, h, mod, wg_b, wu_b, wo_b, conv_w, conv_b.reshape(1, D_FF), ln_g.reshape(1, d), ln_b.reshape(1, d))


def _rope_tables(n):
    rows = n // GRID_W
    row = jnp.repeat(jnp.arange(rows), GRID_W)
    col = jnp.tile(jnp.arange(GRID_W), rows)
    n_freq = D_HEAD // 4
    inv = ROPE_BASE ** (-jnp.arange(n_freq, dtype=F32) / n_freq)
    ang = jnp.concatenate([row[:, None] * inv, col[:, None] * inv], axis=-1)
    ang = jnp.concatenate([jnp.zeros((CTX_LEN, D_HEAD // 2), F32), ang], axis=0)
    cos, sin = jnp.cos(ang), jnp.sin(ang)
    return jnp.concatenate([cos, cos], axis=-1), jnp.concatenate([-sin, sin], axis=-1)


def kernel(x, c, ctx, c_ctx, ada_w, ada_b, w_in, hgrn_lb_logits, hgrn_norm_w, ssm_conv_w, ssm_conv_b, ssm_dt_bias, ssm_a_log, ssm_d, ssm_norm_w, lru_conv_w, lru_conv_b, lru_wa, lru_ba, lru_wi, lru_bi, lru_lambda, ret_decay_logit, w_out, ln1_g, ln1_b, ffn_w_in, ffn_conv_w, ffn_conv_b, ffn_w_out, ln2_g, ln2_b):
    bsz, n, d = x.shape
    assert d == D_MODEL and ctx.shape[1] == CTX_LEN == TM and n % TM == 0 and n % GRID_W == 0
    nb = 2 if bsz % 2 == 0 else 1
    depth = ada_w.shape[0]

    cosf, sinf = _rope_tables(n)
    p_soft = jax.nn.softmax(hgrn_lb_logits.astype(F32), axis=0)
    lb_all = jnp.cumsum(p_soft, axis=0) - p_soft

    rows = -(-(bsz + 1) // 8) * 8
    cond = jnp.zeros((rows, d), F32).at[:bsz].set(c).at[bsz].set(c_ctx)
    mods = _ada_call(cond, ada_w, ada_b)

    n_in = w_in.shape[2]
    dt0 = COL_BXBC + XBC_W
    h = jnp.concatenate([ctx, x], axis=1)
    for l in range(depth):
        ml = mods[l].reshape(rows, 6, d)
        mod = jnp.stack([jnp.broadcast_to(ml[bsz][None], (bsz, 6, d)), ml[:bsz]], axis=1)
        w_l = w_in[l]
        w_perm = jnp.concatenate([w_l[:, :dt0], w_l[:, dt0 + 2 * B_HEADS:], w_l[:, dt0:dt0 + 2 * B_HEADS],
                                  jnp.zeros((d, P_W - n_in), F32)], axis=1).astype(BF16)
        conv_w = jnp.concatenate([ssm_conv_w[l], lru_conv_w[l]], axis=1)
        conv_b = jnp.concatenate([ssm_conv_b[l], lru_conv_b[l]]).reshape(1, CONV_HI - CONV_LO)
        p = _inproj_call(h, mod, w_perm, conv_w, conv_b, nb)
        ya = _mixa_call(p, lb_all[l])
        yb = _mixb_call(p, ssm_dt_bias[l], ssm_a_log[l], ssm_d[l])
        yc = _mixc_call(p, lru_wa[l], lru_ba[l], lru_wi[l], lru_bi[l], lru_lambda[l])
        yd = _mixd_call(p, cosf, sinf, ret_decay_logit[l].astype(F32))
        h = _outproj_call(ya, yb, yc, yd, p, h, mod, w_out[l].astype(BF16), hgrn_norm_w[l], ssm_norm_w[l],
                          ln1_g[l], ln1_b[l], nb)
        wf = ffn_w_in[l]
        h = _ffn_call(h, mod, wf[:, :D_FF].astype(BF16), wf[:, D_FF:].astype(BF16),
                      ffn_w_out[l].astype(BF16), ffn_conv_w[l], ffn_conv_b[l], ln2_g[l], ln2_b[l],
                      nb, skip_ctx=(l == depth - 1))
    return h
```

```python
import functools
import math

import jax
import jax.numpy as jnp
from jax import lax
from jax.experimental import pallas as pl
from jax.experimental.pallas import tpu as pltpu

F32 = jnp.float32
BF16 = jnp.bfloat16

D_MODEL = 1024
DEPTH = 4
GRID_W = 64
CTX_LEN = 256
GROUP_W = 512
A_HEAD = 128
A_HEADS = 4
LB_FLOOR = 1e-30
B_HEADDIM = 64
B_HEADS = 8
B_GROUPS = 2
B_STATE = 128
XBC_W = GROUP_W + 2 * B_GROUPS * B_STATE
C_BLOCKS = 8
C_BLOCK = 64
C_POW = 8.0
D_HEAD = 128
D_HEADS = 4
ROPE_BASE = 10000.0
D_FF = 2816
ALPHA = (2 * DEPTH) ** 0.25
EPS = 1e-6
LOG2E = 1.4426950408889634

TM = 256
FFN_HALO = 8
A_CHUNK = 64
A_SUB = 8
A_FBLK = 16
A_GUARD = 80.0
B_CHUNK = 128
D_CHUNK = 128
FF_CHUNK = 256
FFN_GBUF = 4
VMEM_LIMIT = 56 * 1024 * 1024

COL_AQ, COL_AI, COL_AF, COL_AG = 0, 512, 1024, 2048
COL_BZ, COL_BXBC = 2560, 3072
COL_CX, COL_CG = 4096, 4608
COL_DQ, COL_DK, COL_DV, COL_DG = 5120, 5632, 6144, 6656
COL_DT = 7168
P_W = 7296
DT_W = 128
CONV_LO, CONV_HI = COL_BXBC, COL_CG


def _cparams(sem):
    return pltpu.CompilerParams(dimension_semantics=sem, vmem_limit_bytes=VMEM_LIMIT)


def _softplus(x):
    return jnp.maximum(x, 0.0) + jnp.log1p(jnp.exp(-jnp.abs(x)))


def _sigmoid(x):
    return 1.0 / (1.0 + jnp.exp(-x))


def _silu(x):
    return x * _sigmoid(x)


def _gelu_tanh(x):
    return 0.5 * x * (1.0 + jnp.tanh(math.sqrt(2.0 / math.pi) * (x + 0.044715 * (x * x * x))))


def _dot(a, b):
    return jnp.dot(a, b, preferred_element_type=F32)


def _dot_nt(a, b):
    return lax.dot_general(a, b, (((1,), (1,)), ((), ())), preferred_element_type=F32)


def _split3(x):
    hi = x.astype(BF16)
    r1 = x - hi.astype(F32)
    mid = r1.astype(BF16)
    lo = (r1 - mid.astype(F32)).astype(BF16)
    return hi, mid, lo


def _sel_dot(m01, x):
    hi, mid, lo = _split3(x)
    return _dot(m01, hi) + _dot(m01, mid) + _dot(m01, lo)


def _seq_block(d, j, nblk):
    return jnp.where(d == 0, j, jnp.where(j == 0, 0, nblk - j))


def _ada_kernel(c_ref, w_ref, b_ref, o_ref):
    s = _silu(c_ref[...])
    o_ref[0] = jnp.dot(s, w_ref[0], preferred_element_type=F32,
                       precision=lax.Precision.HIGHEST) + b_ref[0]


def _ada_call(cond, ada_w, ada_b):
    depth, d, n6 = ada_w.shape
    rows = cond.shape[0]
    tn = 1536
    return pl.pallas_call(
        _ada_kernel,
        grid=(depth, n6 // tn),
        in_specs=[pl.BlockSpec((rows, d), lambda l, n: (0, 0)),
                  pl.BlockSpec((1, d, tn), lambda l, n: (l, 0, n)),
                  pl.BlockSpec((1, 1, tn), lambda l, n: (l, 0, n))],
        out_specs=pl.BlockSpec((1, rows, tn), lambda l, n: (l, 0, n)),
        out_shape=jax.ShapeDtypeStruct((depth, rows, n6), F32),
        compiler_params=_cparams(("parallel", "parallel")),
        name="ada_mod",
    )(cond, ada_w, ada_b.reshape(depth, 1, n6))


def _inproj_kernel(h_ref, hp_ref, hn_ref, mod_ref, w_ref, cw_ref, cb_ref, o_ref, g_ref, *, nb, tn, nblk):
    rb = pl.program_id(1)
    prev_ok = rb >= 2
    next_ok = (rb >= 1) & (rb <= nblk - 2)
    m = mod_ref[:, 0]
    sc, sh = 1.0 + m[:, 1:2, :], m[:, 0:1, :]
    hrows = TM + 2 * FFN_HALO
    u_all = (jnp.concatenate([hp_ref[...], h_ref[...], hn_ref[...]], axis=1) * sc + sh).astype(BF16)
    u_all = u_all.reshape(nb * hrows, D_MODEL)
    r1 = lax.broadcasted_iota(jnp.int32, (1, hrows, 1), 1)
    keep = ((r1 >= FFN_HALO) | prev_ok) & ((r1 < FFN_HALO + TM) | next_ok)
    for ci, n0 in enumerate(range(CONV_LO, CONV_HI, tn)):
        cs = slice(n0 - CONV_LO, n0 - CONV_LO + tn)
        g = _dot(u_all, w_ref[:, n0:n0 + tn]).reshape(nb, hrows, tn)
        g_ref[ci] = jnp.where(keep, g, 0.0)
        acc = cb_ref[:, cs].reshape(1, 1, tn)
        for jj in range(4):
            acc = acc + cw_ref[jj:jj + 1, cs].reshape(1, 1, tn) * g_ref[ci, :, pl.ds(FFN_HALO - 2 + jj, TM), :]
        if n0 < COL_CX:
            acc = _silu(acc)
        o_ref[:, :, n0:n0 + tn] = acc.astype(BF16)
    ub = (h_ref[...] * sc + sh).astype(BF16).reshape(nb * TM, D_MODEL)
    for n0 in list(range(0, CONV_LO, tn)) + list(range(CONV_HI, P_W, tn)):
        w = min(tn, P_W - n0)
        o_ref[:, :, n0:n0 + w] = _dot(ub, w_ref[:, n0:n0 + w]).astype(BF16).reshape(nb, TM, w)


def _inproj_call(h, mod, w_in_b, conv_w, conv_b, nb):
    b, l, d = h.shape
    nblk = l // TM
    per = TM // FFN_HALO
    last = l // FFN_HALO - 1
    ncv = CONV_HI - CONV_LO
    kern = functools.partial(_inproj_kernel, nb=nb, tn=512, nblk=nblk)
    return pl.pallas_call(
        kern,
        grid=(b // nb, nblk),
        in_specs=[pl.BlockSpec((nb, TM, d), lambda i, t: (i, t, 0)),
                  pl.BlockSpec((nb, FFN_HALO, d), lambda i, t: (i, jnp.maximum(t * per - 1, 0), 0)),
                  pl.BlockSpec((nb, FFN_HALO, d), lambda i, t: (i, jnp.minimum((t + 1) * per, last), 0)),
                  pl.BlockSpec((nb, 1, 6, d), lambda i, t: (i, jnp.minimum(t, 1), 0, 0)),
                  pl.BlockSpec((d, P_W), lambda i, t: (0, 0), pipeline_mode=pl.Buffered(1)),
                  pl.BlockSpec((4, ncv), lambda i, t: (0, 0)),
                  pl.BlockSpec((1, ncv), lambda i, t: (0, 0))],
        out_specs=pl.BlockSpec((nb, TM, P_W), lambda i, t: (i, t, 0)),
        out_shape=jax.ShapeDtypeStruct((b, l, P_W), BF16),
        scratch_shapes=[pltpu.VMEM((ncv // 512, nb, TM + 2 * FFN_HALO, 512), F32)],
        compiler_params=_cparams(("parallel", "parallel")),
        name="in_proj",
    )(h, h, h, mod, w_in_b, conv_w, conv_b)


def _mixa_body(q_ref, v_ref, f_ref, lb_ref, y_ref, st_ref, sh_ref, rev):
    C, m = A_CHUNK, A_SUB
    nsl = C // m - 1
    lbv = lb_ref[0]
    log_lb = jnp.log(jnp.maximum(lbv, LB_FLOOR))
    log_1mlb = jnp.log1p(-lbv)
    one_m_lb = 1.0 - lbv

    row = lax.broadcasted_iota(jnp.int32, (C, C), 0)
    col = lax.broadcasted_iota(jnp.int32, (C, C), 1)
    tau_r = (C - 1 - row) if rev else row
    tau_c = (C - 1 - col) if rev else col
    blk_r, blk_c = tau_r // m, tau_c // m
    tri = (tau_c <= tau_r).astype(BF16)
    refrow = (C - m * blk_r) if rev else (m * blk_r - 1)
    rsel = ((col == refrow) & (blk_r >= 1)).astype(BF16)
    off_mask = blk_c < blk_r
    r1 = lax.broadcasted_iota(jnp.int32, (C, 1), 0)
    tau1 = (C - 1 - r1) if rev else r1
    blk1 = tau1 // m
    off1 = tau1 % m

    pad = m
    for c in (range(TM // C - 1, -1, -1) if rev else range(TM // C)):
        sl = pl.ds(c * C, C)
        q = q_ref[0, sl, :].astype(F32)
        v = v_ref[0, sl, :].astype(F32)
        u = f_ref[0, sl, :].astype(F32)
        log_sig = jnp.minimum(u, 0.0) - jnp.log1p(jnp.exp(-jnp.abs(u)))
        t2 = log_1mlb + log_sig
        log_f = jnp.maximum(log_lb, t2) + jnp.log1p(jnp.exp(-jnp.abs(log_lb - t2)))
        k = one_m_lb * _sigmoid(-u)
        b = _sel_dot(tri, log_f)
        rr = _sel_dot(rsel, b)
        qt = q * jnp.exp(jnp.minimum(b - rr, 0.0))
        sh_ref[0, pl.ds(pad, C), :] = k
        sh_ref[1, pl.ds(pad, C), :] = b
        sh_ref[2, pl.ds(pad, C), :] = v
        y_parts = []
        for h in range(A_HEADS):
            hs = slice(h * A_HEAD, (h + 1) * A_HEAD)
            qh, kh, vh, bh, qth = q[:, hs], k[:, hs], v[:, hs], b[:, hs], qt[:, hs]
            tot = bh[0:1, :] if rev else bh[C - 1:C, :]
            q_aug = jnp.concatenate(
                [jnp.where(blk1 == i, qth, 0.0) for i in range(1, nsl + 1)], axis=1).astype(BF16)
            k_parts = []
            for i in range(1, nsl + 1):
                rrow = (C - m * i) if rev else (m * i - 1)
                k_parts.append(kh * jnp.exp(jnp.minimum(bh[rrow:rrow + 1, :] - bh, 0.0)))
            k_aug = jnp.concatenate(k_parts, axis=1).astype(BF16)
            p_off = jnp.where(off_mask, _dot_nt(q_aug, k_aug), 0.0)
            y = _dot(p_off.astype(BF16), vh.astype(BF16))
            for dl in range(m):
                if dl == 0:
                    ks, bs, vs = kh, bh, vh
                else:
                    st = pad + dl if rev else pad - dl
                    ks = sh_ref[0, pl.ds(st, C), hs]
                    bs = sh_ref[1, pl.ds(st, C), hs]
                    vs = sh_ref[2, pl.ds(st, C), hs]
                tmp = qh * ks * jnp.exp(jnp.minimum(bh - bs, 0.0))
                p = jnp.sum(tmp, axis=-1, keepdims=True)
                y = y + jnp.where(off1 >= dl, p, 0.0) * vs
            st_t = st_ref[h]
            y = y + _dot_nt((qh * jnp.exp(bh)).astype(BF16), st_t.astype(BF16))
            kdec = (kh * jnp.exp(tot - bh)).astype(BF16)
            st_ref[h] = st_t * jnp.exp(tot) + _dot(vh.T.astype(BF16), kdec)
            y_parts.append(y)
        y_ref[0, 0, sl, :] = jnp.concatenate(y_parts, axis=1).astype(BF16)


def _mixa_tables(tm_ref, code_ref, rev):
    C, m = A_CHUNK, A_FBLK
    row = lax.broadcasted_iota(jnp.int32, (TM, TM), 0)
    col = lax.broadcasted_iota(jnp.int32, (TM, TM), 1)
    same_chunk = (row // C) == (col // C)
    tr, tc = row % C, col % C
    if rev:
        tr, tc = C - 1 - tr, C - 1 - tc
    same_blk = same_chunk & ((tr // m) == (tc // m))
    orr, oc = tr % m, tc % m
    half = m // 2
    tm_ref[0] = (same_chunk & (tc <= tr)).astype(BF16)
    tm_ref[1] = (same_blk & (oc <= orr)).astype(BF16)
    plus = same_blk & (oc >= half) & (oc <= orr)
    minus = same_blk & (oc > orr) & (oc < half)
    tm_ref[2] = (plus.astype(F32) - minus.astype(F32)).astype(BF16)
    off = same_chunk & ((tc // m) < (tr // m))
    diag = same_blk & (tc <= tr)
    code_ref[...] = jnp.where(off, 1.0, jnp.where(diag, 2.0, 0.0))


def _mixa_fast(q_ref, v_ref, y_ref, st_ref, code_ref, k_ref, b_ref, bl_ref, bm_ref, rev):
    C, m = A_CHUNK, A_FBLK
    nch = TM // C
    nsl = C // m - 1
    code = code_ref[...]
    off_mask = code == 1.0
    diag_mask = code == 2.0
    r1 = lax.broadcasted_iota(jnp.int32, (TM, 1), 0) % C
    blk1 = ((C - 1 - r1) if rev else r1) // m
    rows = lambda a, c: a[c * C:(c + 1) * C, :]
    order = range(nch - 1, -1, -1) if rev else range(nch)
    y_parts = []
    for h in range(A_HEADS):
        hs = slice(h * A_HEAD, (h + 1) * A_HEAD)
        qh = q_ref[0, :, hs].astype(F32)
        vb = v_ref[0, :, hs]
        kh, bh, blh, bmh = k_ref[:, hs], b_ref[:, hs], bl_ref[:, hs], bm_ref[:, hs]
        qt = qh * jnp.exp2(blh)
        q_aug = jnp.concatenate(
            [jnp.where(blk1 == i, qt, 0.0) for i in range(1, nsl + 1)], axis=1).astype(BF16)
        k_parts = []
        for i in range(1, nsl + 1):
            rrow = (C - m * i) if rev else (m * i - 1)
            dif = jnp.concatenate([bh[c * C + rrow:c * C + rrow + 1, :] - rows(bh, c) for c in range(nch)], axis=0)
            k_parts.append(kh * jnp.exp2(jnp.minimum(dif, 0.0)))
        k_aug = jnp.concatenate(k_parts, axis=1).astype(BF16)
        p = jnp.where(off_mask, _dot_nt(q_aug, k_aug), 0.0)
        qd = (qh * jnp.exp2(bmh)).astype(BF16)
        kd = (kh * jnp.exp2(-bmh)).astype(BF16)
        p = p + jnp.where(diag_mask, _dot_nt(qd, kd), 0.0)
        y = _dot(p.astype(BF16), vb)
        qe = (qh * jnp.exp2(bh)).astype(BF16)
        tots = [bh[c * C:c * C + 1, :] if rev else bh[(c + 1) * C - 1:(c + 1) * C, :] for c in range(nch)]
        upd = [_dot(rows(vb, c).astype(F32).T.astype(BF16),
                    (rows(kh, c) * jnp.exp2(tots[c] - rows(bh, c))).astype(BF16)) for c in range(nch)]
        st = st_ref[h]
        y_inter = [None] * nch
        for c in order:
            y_inter[c] = _dot_nt(rows(qe, c), st.astype(BF16))
            st = st * jnp.exp2(tots[c]) + upd[c]
        st_ref[h] = st
        y_parts.append(y + jnp.concatenate(y_inter, axis=0))
    y_ref[0, 0] = jnp.concatenate(y_parts, axis=1).astype(BF16)


def _mixa_prepare(f_ref, lb_ref, tm_ref, k_ref, b_ref, bl_ref, bm_ref):
    lbv = lb_ref[0]
    lb_floor = jnp.maximum(lbv, LB_FLOOR)
    one_m_lb = 1.0 - lbv
    worst = jnp.zeros((1, 1), F32)
    lw = 2 * A_HEAD
    for h in range(GROUP_W // lw):
        hs = slice(h * lw, (h + 1) * lw)
        u = f_ref[0, :, hs].astype(F32)
        e = jnp.exp(-jnp.abs(u))
        big = 1.0 / (1.0 + e)
        small = e * big
        pos = u >= 0.0
        sig_u = jnp.where(pos, big, small)
        sig_mu = jnp.where(pos, small, big)
        k_ref[:, hs] = one_m_lb[:, hs] * sig_mu
        lf2 = jnp.log2(lb_floor[:, hs] + one_m_lb[:, hs] * sig_u)
        hi = lf2.astype(BF16)
        lo = (lf2 - hi.astype(F32)).astype(BF16)
        cum = lambda i: _dot(tm_ref[i], hi) + _dot(tm_ref[i], lo)
        b_ref[:, hs] = cum(0)
        bl_ref[:, hs] = cum(1)
        bm = cum(2)
        bm_ref[:, hs] = bm
        worst = jnp.maximum(worst, jnp.max(jnp.abs(bm), keepdims=True))
    return worst[0, 0] <= A_GUARD * LOG2E


def _mixb_body(xc_ref, dt_ref, dtb_ref, alog_ref, e_ref, dsk_ref, y_ref, st_ref, rev, d):
    C = B_CHUNK
    hpg = B_HEADS // B_GROUPS
    gw = hpg * B_HEADDIM
    row = lax.broadcasted_iota(jnp.int32, (C, C), 0)
    col = lax.broadcasted_iota(jnp.int32, (C, C), 1)
    trimask = (col >= row) if rev else (col <= row)
    tri = trimask.astype(BF16)
    lane_head = lax.broadcasted_iota(jnp.int32, (1, gw), 1) // B_HEADDIM
    e01 = e_ref[0]
    dtb = dtb_ref[0]
    neg_a = -jnp.exp(alog_ref[0])

    def expand(x):
        hi, mid, lo = _split3(x)
        return _dot(hi, e01) + _dot(mid, e01) + _dot(lo, e01)

    for c in (range(TM // C - 1, -1, -1) if rev else range(TM // C)):
        sl = pl.ds(c * C, C)
        bx = xc_ref[0, sl, 0:GROUP_W].astype(F32)
        bb = xc_ref[0, sl, GROUP_W:GROUP_W + B_GROUPS * B_STATE]
        bc = xc_ref[0, sl, GROUP_W + B_GROUPS * B_STATE:XBC_W]
        dt = _softplus(dt_ref[0, sl, :].astype(F32) + dtb)
        la = dt * neg_a
        b = _sel_dot(tri, la)
        b_exp = expand(b)
        dt_exp = expand(dt)
        tot_exp = b_exp[0:1, :] if rev else b_exp[C - 1:C, :]
        v_all = bx * dt_exp
        vdec = v_all * jnp.exp(tot_exp - b_exp)
        eb = jnp.exp(b_exp)
        bt = b.T
        y_parts = []
        for g in range(B_GROUPS):
            gs = slice(g * gw, (g + 1) * gw)
            bg = bb[:, g * B_STATE:(g + 1) * B_STATE]
            cg = bc[:, g * B_STATE:(g + 1) * B_STATE]
            qk = _dot_nt(cg, bg)
            s_g = st_ref[g]
            vg = v_all[:, gs].astype(BF16)
            m_parts, v_parts = [], []
            for e in range(hpg):
                lane = 8 * d + g * hpg + e
                rel = jnp.where(trimask,
                                jnp.exp(jnp.minimum(b[:, lane:lane + 1] - bt[lane:lane + 1, :], 0.0)), 0.0)
                m_parts.append((qk * rel).astype(BF16))
                v_parts.append(jnp.where(lane_head == e, vg, jnp.zeros_like(vg)))
            y_g = (_dot(jnp.concatenate(m_parts, axis=1), jnp.concatenate(v_parts, axis=0))
                   + _dot(cg, s_g.astype(BF16)) * eb[:, gs])
            st_ref[g] = (s_g * jnp.exp(tot_exp[:, gs])
                         + _dot(bg.astype(F32).T.astype(BF16), vdec[:, gs].astype(BF16)))
            y_parts.append(y_g)
        y = jnp.concatenate(y_parts, axis=1)
        if not rev:
            y = y + dsk_ref[...] * bx
        y_ref[0, 0, sl, :] = y.astype(BF16)


def _mixc_body(y_ref, carry_ref, a_ref, b_ref, rev):
    grp = TM // 8
    a3 = a_ref[...].reshape(grp, 8, GROUP_W)
    b3 = b_ref[...].reshape(grp, 8, GROUP_W)
    ro = lax.broadcasted_iota(jnp.int32, (1, 8, 1), 1)
    for s in (1, 2, 4):
        valid = (ro + s <= 7) if rev else (ro >= s)
        shift = (8 - s) if rev else s
        a_sh = pltpu.roll(a3, shift, 1)
        b_sh = pltpu.roll(b3, shift, 1)
        b3 = b3 + a3 * jnp.where(valid, b_sh, 0.0)
        a3 = a3 * jnp.where(valid, a_sh, 1.0)
    cb = jnp.broadcast_to(carry_ref[...], (8, GROUP_W))
    last = 0 if rev else 7
    hs = [None] * grp
    for g in (range(grp - 1, -1, -1) if rev else range(grp)):
        hg = a3[g] * cb + b3[g]
        hs[g] = hg
        cb = jnp.broadcast_to(hg[last:last + 1, :], (8, GROUP_W))
    carry_ref[...] = cb[0:1, :]
    y_ref[0, 0] = jnp.concatenate(hs, axis=0).astype(BF16)


def _mixc_gates(x_ref, wa_ref, wi_ref, ba_ref, bi_ref, lam_ref, a_ref, b_ref):
    cxb = x_ref[0]
    cx = cxb.astype(F32)
    r = _sigmoid(_dot(cxb, wa_ref[0]) + ba_ref[0])
    gi = _sigmoid(_dot(cxb, wi_ref[0]) + bi_ref[0])
    la = (-C_POW) * r * _softplus(-lam_ref[0])
    a = jnp.exp(la)
    a_ref[...] = a
    b_ref[...] = jnp.sqrt(1.0 - a * a) * gi * cx


def _block_diag(w):
    nd, nb_, n, _ = w.shape
    eye = jnp.eye(nb_, dtype=w.dtype)
    return jnp.einsum("dhij,hg->dhigj", w, eye).reshape(nd, nb_ * n, nb_ * n)


def _mixd_body(q_ref, k_ref, v_ref, cos_ref, sin_ref, lgt_ref, y_ref, st_ref, rev):
    C = D_CHUNK
    lg = -_softplus(-lgt_ref[0])
    row = lax.broadcasted_iota(jnp.int32, (C, C), 0)
    col = lax.broadcasted_iota(jnp.int32, (C, C), 1)
    trimask = (col >= row) if rev else (col <= row)
    dist = jnp.maximum((col - row) if rev else (row - col), 0).astype(F32)
    r1 = lax.broadcasted_iota(jnp.int32, (C, 1), 0)
    tau1 = ((C - 1 - r1) if rev else r1).astype(F32)
    scale = D_HEAD ** -0.5
    for c in (range(TM // C - 1, -1, -1) if rev else range(TM // C)):
        sl = pl.ds(c * C, C)
        cosf = cos_ref[sl, :]
        sinf = sin_ref[sl, :]
        y_parts = []
        for h in range(D_HEADS):
            hs = slice(h * D_HEAD, (h + 1) * D_HEAD)
            lgh = lg[h:h + 1, 0:1]
            qh = q_ref[0, sl, hs].astype(F32)
            kh = k_ref[0, sl, hs].astype(F32)
            vh = v_ref[0, sl, hs]
            qr = qh * cosf + pltpu.roll(qh, D_HEAD // 2, 1) * sinf
            kr = (kh * cosf + pltpu.roll(kh, D_HEAD // 2, 1) * sinf) * scale
            qb = qr.astype(BF16)
            qk = _dot_nt(qb, kr.astype(BF16))
            rel = jnp.where(trimask, jnp.exp(lgh * dist), 0.0)
            s_h = st_ref[h]
            y = _dot((qk * rel).astype(BF16), vh) + _dot(qb, s_h.astype(BF16)) * jnp.exp(lgh * (tau1 + 1.0))
            kdec = kr * jnp.exp(lgh * (C - 1.0 - tau1))
            st_ref[h] = s_h * jnp.exp(lgh * float(C)) + _dot(kdec.T.astype(BF16), vh)
            y_parts.append(y)
        y_ref[0, 0, sl, :] = jnp.concatenate(y_parts, axis=1).astype(BF16)


def _mix_kernel(aq_ref, ai_ref, af_ref, lb_ref,
                bx_ref, bdt_ref, dtb_ref, alog_ref, e_ref, dsk_ref,
                cx_ref, wa_ref, wi_ref, ba_ref, bi_ref, lam_ref,
                dq_ref, dk_ref, dv_ref, cos_ref, sin_ref, lgt_ref,
                ya_ref, yb_ref, yc_ref, yd_ref,
                a_st, a_sh, a_tm, a_code, a_k, a_b, a_bl, a_bm, b_st, c_carry, c_a, c_b, d_st):
    d = pl.program_id(1)
    j = pl.program_id(2)

    @pl.when(j == 0)
    def _():
        for r in (a_st, a_sh, b_st, c_carry, d_st):
            r[...] = jnp.zeros_like(r)

    def run(rev, dd):
        @pl.when(j == 0)
        def _():
            _mixa_tables(a_tm, a_code, rev)

        ok = _mixa_prepare(af_ref, lb_ref, a_tm, a_k, a_b, a_bl, a_bm)
        _mixb_body(bx_ref, bdt_ref, dtb_ref, alog_ref, e_ref, dsk_ref, yb_ref, b_st, rev, dd)
        _mixc_gates(cx_ref, wa_ref, wi_ref, ba_ref, bi_ref, lam_ref, c_a, c_b)
        _mixc_body(yc_ref, c_carry, c_a, c_b, rev)
        _mixd_body(dq_ref, dk_ref, dv_ref, cos_ref, sin_ref, lgt_ref, yd_ref, d_st, rev)

        @pl.when(ok)
        def _():
            _mixa_fast(aq_ref, ai_ref, ya_ref, a_st, a_code, a_k, a_b, a_bl, a_bm, rev)

        @pl.when(jnp.logical_not(ok))
        def _():
            _mixa_body(aq_ref, ai_ref, af_ref, lb_ref, ya_ref, a_st, a_sh, rev)

    @pl.when(d == 0)
    def _():
        run(False, 0)

    @pl.when(d == 1)
    def _():
        run(True, 1)


def _mix_call(p, lb, dt_bias, a_log, ssm_d, wa, ba, wi, bi, lam, cosf, sinf, decay_logit):
    b, l, _ = p.shape
    nblk = l // TM
    rowblk = lambda d, j: _seq_block(d, j, nblk)
    pcol = lambda col, width=GROUP_W: pl.BlockSpec((1, TM, width), lambda i, d, j: (i, rowblk(d, j), col // width))
    dirvec = lambda width: pl.BlockSpec((1, 1, width), lambda i, d, j: (d, 0, 0))
    pad_dir = lambda a: jnp.stack([jnp.pad(a[0], (0, DT_W - B_HEADS)),
                                   jnp.pad(a[1], (B_HEADS, DT_W - 2 * B_HEADS))]).reshape(2, 1, DT_W)
    lanes = jnp.arange(GROUP_W) // B_HEADDIM
    e01 = jnp.stack([(jnp.arange(DT_W)[:, None] == lanes[None, :] + 8 * dd) for dd in range(2)]).astype(BF16)
    dsk = jnp.repeat(ssm_d, B_HEADDIM).reshape(1, GROUP_W)
    vec = lambda a: a.reshape(2, 1, GROUP_W)
    wspec = pl.BlockSpec((1, GROUP_W, GROUP_W), lambda i, d, j: (d, 0, 0))
    lgt = jnp.broadcast_to(jnp.pad(decay_logit, ((0, 0), (0, 8 - D_HEADS)))[:, :, None], (2, 8, D_HEAD))
    tab = pl.BlockSpec((TM, D_HEAD), lambda i, d, j: (rowblk(d, j), 0))
    yspec = pl.BlockSpec((1, 1, TM, GROUP_W), lambda i, d, j: (d, i, rowblk(d, j), 0))
    yshape = jax.ShapeDtypeStruct((2, b, l, GROUP_W), BF16)
    act = lambda: pltpu.VMEM((TM, GROUP_W), F32)
    return pl.pallas_call(
        _mix_kernel,
        grid=(b, 2, nblk),
        in_specs=[pcol(COL_AQ), pcol(COL_AI),
                  pl.BlockSpec((1, TM, GROUP_W), lambda i, d, j: (i, rowblk(d, j), COL_AF // GROUP_W + d)),
                  dirvec(GROUP_W),
                  pcol(COL_BXBC, XBC_W), pcol(COL_DT, DT_W), dirvec(DT_W), dirvec(DT_W),
                  pl.BlockSpec((1, DT_W, GROUP_W), lambda i, d, j: (d, 0, 0)),
                  pl.BlockSpec((1, GROUP_W), lambda i, d, j: (0, 0)),
                  pcol(COL_CX), wspec, wspec, dirvec(GROUP_W), dirvec(GROUP_W), dirvec(GROUP_W),
                  pcol(COL_DQ), pcol(COL_DK), pcol(COL_DV), tab, tab,
                  pl.BlockSpec((1, 8, D_HEAD), lambda i, d, j: (d, 0, 0))],
        out_specs=[yspec, yspec, yspec, yspec],
        out_shape=[yshape, yshape, yshape, yshape],
        scratch_shapes=[pltpu.VMEM((A_HEADS, A_HEAD, A_HEAD), F32),
                        pltpu.VMEM((3, A_CHUNK + 2 * A_SUB, GROUP_W), F32),
                        pltpu.VMEM((3, TM, TM), BF16),
                        pltpu.VMEM((TM, TM), F32),
                        act(), act(), act(), act(),
                        pltpu.VMEM((B_GROUPS, B_STATE, GROUP_W // B_GROUPS), F32),
                        pltpu.VMEM((1, GROUP_W), F32), act(), act(),
                        pltpu.VMEM((D_HEADS, D_HEAD, D_HEAD), F32)],
        compiler_params=_cparams(("parallel", "parallel", "arbitrary")),
        name="mixers",
    )(p, p, p, lb.reshape(2, 1, GROUP_W),
      p, p, pad_dir(dt_bias), pad_dir(a_log), e01, dsk,
      p, _block_diag(wa).astype(BF16), _block_diag(wi).astype(BF16), vec(ba), vec(bi), vec(lam),
      p, p, p, cosf, sinf, lgt)


def _layer_norm_rows(x, g, b):
    mu = jnp.mean(x, axis=-1, keepdims=True)
    xc = x - mu
    var = jnp.mean(xc * xc, axis=-1, keepdims=True)
    return xc * lax.rsqrt(var + EPS) * g + b


def _outproj_kernel(yaf, yab, ybf, ybb, ycf, ycb, ydf, ydb, ag_ref, bz_ref, cg_ref, dg_ref,
                    h_ref, mod_ref, w_ref, nwa_ref, nwb_ref, lg_ref, lb_ref, o_ref, *, nb):
    rows = nb * TM
    ld = lambda f, bk: (f[0].astype(F32) + bk[0].astype(F32)).reshape(rows, GROUP_W)
    gate = lambda r: r[...].astype(F32).reshape(rows, GROUP_W)
    ya = ld(yaf, yab)
    parts = []
    for h in range(A_HEADS):
        x = ya[:, h * A_HEAD:(h + 1) * A_HEAD]
        parts.append(x * lax.rsqrt(jnp.mean(x * x, axis=-1, keepdims=True) + EPS))
    oa = jnp.concatenate(parts, axis=1) * nwa_ref[...] * _silu(gate(ag_ref))
    yb = ld(ybf, ybb) * _silu(gate(bz_ref))
    gw = GROUP_W // B_GROUPS
    parts = []
    for g in range(B_GROUPS):
        x = yb[:, g * gw:(g + 1) * gw]
        parts.append(x * lax.rsqrt(jnp.mean(x * x, axis=-1, keepdims=True) + EPS))
    ob = jnp.concatenate(parts, axis=1) * nwb_ref[...]
    oc = ld(ycf, ycb) * _gelu_tanh(gate(cg_ref))
    yd = ld(ydf, ydb)
    parts = []
    for h in range(D_HEADS):
        x = yd[:, h * D_HEAD:(h + 1) * D_HEAD]
        mu = jnp.mean(x, axis=-1, keepdims=True)
        xc = x - mu
        parts.append(xc * lax.rsqrt(jnp.mean(xc * xc, axis=-1, keepdims=True) + EPS))
    od = jnp.concatenate(parts, axis=1) * _silu(gate(dg_ref))
    mix = (_dot(oa.astype(BF16), w_ref[0:GROUP_W, :])
           + _dot(ob.astype(BF16), w_ref[GROUP_W:2 * GROUP_W, :])
           + _dot(oc.astype(BF16), w_ref[2 * GROUP_W:3 * GROUP_W, :])
           + _dot(od.astype(BF16), w_ref[3 * GROUP_W:4 * GROUP_W, :]))
    m = mod_ref[:, 0]
    hn = ALPHA * h_ref[...] + m[:, 2:3, :] * mix.reshape(nb, TM, D_MODEL)
    o_ref[...] = _layer_norm_rows(hn, lg_ref[...], lb_ref[...])


def _outproj_call(ya, yb, yc, yd, p, h, mod, w_out_b, nwa, nwb, ln_g, ln_b, nb):
    b, l, d = h.shape
    ydir = lambda dd: pl.BlockSpec((1, nb, TM, GROUP_W), lambda i, t: (dd, i, t, 0))
    pcol = lambda col: pl.BlockSpec((nb, TM, GROUP_W), lambda i, t: (i, t, col // GROUP_W))
    vec = lambda n: pl.BlockSpec((1, n), lambda i, t: (0, 0))
    return pl.pallas_call(
        functools.partial(_outproj_kernel, nb=nb),
        grid=(b // nb, l // TM),
        in_specs=[ydir(0), ydir(1), ydir(0), ydir(1), ydir(0), ydir(1), ydir(0), ydir(1),
                  pcol(COL_AG), pcol(COL_BZ), pcol(COL_CG), pcol(COL_DG),
                  pl.BlockSpec((nb, TM, d), lambda i, t: (i, t, 0)),
                  pl.BlockSpec((nb, 1, 6, d), lambda i, t: (i, jnp.minimum(t, 1), 0, 0)),
                  pl.BlockSpec((4 * GROUP_W, d), lambda i, t: (0, 0), pipeline_mode=pl.Buffered(1)),
                  vec(GROUP_W), vec(GROUP_W), vec(d), vec(d)],
        out_specs=pl.BlockSpec((nb, TM, d), lambda i, t: (i, t, 0)),
        out_shape=jax.ShapeDtypeStruct((b, l, d), F32),
        compiler_params=_cparams(("parallel", "parallel")),
        name="out_proj",
    )(ya, ya, yb, yb, yc, yc, yd, yd, p, p, p, p, h, mod, w_out_b,
      nwa.reshape(1, GROUP_W), nwb.reshape(1, GROUP_W), ln_g.reshape(1, d), ln_b.reshape(1, d))


def _ffn_kernel(h_ref, hp_ref, hn_ref, mod_ref, wg_ref, wu_ref, wo_ref, cw_ref, cb_ref, lg_ref, lb_ref,
                o_ref, act_ref, g_ref, *, nb, nblk, first_blk):
    rb = pl.program_id(1) + first_blk
    prev_ok = rb >= 2
    next_ok = (rb >= 1) & (rb <= nblk - 2)
    m = mod_ref[:, 0]
    sh, sc = m[:, 3:4, :], m[:, 4:5, :]
    hm = h_ref[...]
    hrows = TM + 2 * FFN_HALO
    x_all = jnp.concatenate([hp_ref[...], hm, hn_ref[...]], axis=1) * (1.0 + sc) + sh
    xb_all = x_all.astype(BF16).reshape(nb * hrows, D_MODEL)
    xb = x_all[:, FFN_HALO:FFN_HALO + TM, :].astype(BF16).reshape(nb * TM, D_MODEL)
    r1 = lax.broadcasted_iota(jnp.int32, (1, hrows, 1), 1)
    keep = ((r1 >= FFN_HALO) | prev_ok) & ((r1 < FFN_HALO + TM) | next_ok)
    for c0 in range(0, D_FF, FF_CHUNK):
        cs = slice(c0, c0 + FF_CHUNK)
        g = _dot(xb_all, wg_ref[:, cs]).reshape(nb, hrows, FF_CHUNK)
        gi = (c0 // FF_CHUNK) % FFN_GBUF
        g_ref[gi] = jnp.where(keep, g, 0.0)
        gc = cb_ref[:, cs].reshape(1, 1, FF_CHUNK)
        for jj in range(3):
            gc = gc + (cw_ref[jj:jj + 1, cs].reshape(1, 1, FF_CHUNK)
                       * g_ref[gi, :, pl.ds(FFN_HALO - 1 + jj, TM), :])
        up = _dot(xb, wu_ref[:, cs])
        act_ref[:, cs] = (_gelu_tanh(gc).reshape(nb * TM, FF_CHUNK) * up).astype(BF16)
    f = _dot(act_ref[...], wo_ref[...])
    hn2 = ALPHA * hm + m[:, 5:6, :] * f.reshape(nb, TM, D_MODEL)
    o_ref[...] = _layer_norm_rows(hn2, lg_ref[...], lb_ref[...])


def _ffn_call(h, mod, wg_b, wu_b, wo_b, conv_w, conv_b, ln_g, ln_b, nb, skip_ctx):
    b, l, d = h.shape
    nblk = l // TM
    first = 1 if skip_ctx else 0
    per = TM // FFN_HALO
    last = l // FFN_HALO - 1
    vec = lambda n: pl.BlockSpec((1, n), lambda i, t: (0, 0))
    res = lambda shape: pl.BlockSpec(shape, lambda i, t: (0, 0), pipeline_mode=pl.Buffered(1))
    return pl.pallas_call(
        functools.partial(_ffn_kernel, nb=nb, nblk=nblk, first_blk=first),
        grid=(b // nb, nblk - first),
        in_specs=[pl.BlockSpec((nb, TM, d), lambda i, t: (i, t + first, 0)),
                  pl.BlockSpec((nb, FFN_HALO, d), lambda i, t: (i, jnp.maximum((t + first) * per - 1, 0), 0)),
                  pl.BlockSpec((nb, FFN_HALO, d), lambda i, t: (i, jnp.minimum((t + first + 1) * per, last), 0)),
                  pl.BlockSpec((nb, 1, 6, d), lambda i, t: (i, jnp.minimum(t + first, 1), 0, 0)),
                  res((d, D_FF)), res((d, D_FF)), res((D_FF, d)),
                  pl.BlockSpec((3, D_FF), lambda i, t: (0, 0)), vec(D_FF), vec(d), vec(d)],
        out_specs=pl.BlockSpec((nb, TM, d), lambda i, t: (i, t, 0)),
        out_shape=jax.ShapeDtypeStruct((b, l - first * TM, d), F32),
        scratch_shapes=[pltpu.VMEM((nb * TM, D_FF), BF16),
                        pltpu.VMEM((FFN_GBUF, nb, TM + 2 * FFN_HALO, FF_CHUNK), F32)],
        compiler_params=_cparams(("parallel", "parallel")),
        name="conv_ffn",
    )(h, h, h, mod, wg_b, wu_b, wo_b, conv_w, conv_b.reshape(1, D_FF), ln_g.reshape(1, d), ln_b.reshape(1, d))


def _rope_tables(n):
    rows = n // GRID_W
    row = jnp.repeat(jnp.arange(rows), GRID_W)
    col = jnp.tile(jnp.arange(GRID_W), rows)
    n_freq = D_HEAD // 4
    inv = ROPE_BASE ** (-jnp.arange(n_freq, dtype=F32) / n_freq)
    ang = jnp.concatenate([row[:, None] * inv, col[:, None] * inv], axis=-1)
    ang = jnp.concatenate([jnp.zeros((CTX_LEN, D_HEAD // 2), F32), ang], axis=0)
    cos, sin = jnp.cos(ang), jnp.sin(ang)
    return jnp.concatenate([cos, cos], axis=-1), jnp.concatenate([-sin, sin], axis=-1)


def kernel(x, c, ctx, c_ctx, ada_w, ada_b, w_in, hgrn_lb_logits, hgrn_norm_w, ssm_conv_w, ssm_conv_b, ssm_dt_bias, ssm_a_log, ssm_d, ssm_norm_w, lru_conv_w, lru_conv_b, lru_wa, lru_ba, lru_wi, lru_bi, lru_lambda, ret_decay_logit, w_out, ln1_g, ln1_b, ffn_w_in, ffn_conv_w, ffn_conv_b, ffn_w_out, ln2_g, ln2_b):
    bsz, n, d = x.shape
    assert d == D_MODEL and ctx.shape[1] == CTX_LEN == TM and n % TM == 0 and n % GRID_W == 0
    nb = 2 if bsz % 2 == 0 else 1
    depth = ada_w.shape[0]

    cosf, sinf = _rope_tables(n)
    p_soft = jax.nn.softmax(hgrn_lb_logits.astype(F32), axis=0)
    lb_all = jnp.cumsum(p_soft, axis=0) - p_soft

    rows = -(-(bsz + 1) // 8) * 8
    cond = jnp.zeros((rows, d), F32).at[:bsz].set(c).at[bsz].set(c_ctx)
    mods = _ada_call(cond, ada_w, ada_b)

    n_in = w_in.shape[2]
    dt0 = COL_BXBC + XBC_W
    h = jnp.concatenate([ctx, x], axis=1)
    for l in range(depth):
        ml = mods[l].reshape(rows, 6, d)
        mod = jnp.stack([jnp.broadcast_to(ml[bsz][None], (bsz, 6, d)), ml[:bsz]], axis=1)
        w_l = w_in[l]
        w_perm = jnp.concatenate([w_l[:, :dt0], w_l[:, dt0 + 2 * B_HEADS:], w_l[:, dt0:dt0 + 2 * B_HEADS],
                                  jnp.zeros((d, P_W - n_in), F32)], axis=1).astype(BF16)
        conv_w = jnp.concatenate([ssm_conv_w[l], lru_conv_w[l]], axis=1)
        conv_b = jnp.concatenate([ssm_conv_b[l], lru_conv_b[l]]).reshape(1, CONV_HI - CONV_LO)
        p = _inproj_call(h, mod, w_perm, conv_w, conv_b, nb)
        ya, yb, yc, yd = _mix_call(p, lb_all[l], ssm_dt_bias[l], ssm_a_log[l], ssm_d[l],
                                   lru_wa[l], lru_ba[l], lru_wi[l], lru_bi[l], lru_lambda[l],
                                   cosf, sinf, ret_decay_logit[l].astype(F32))
        h = _outproj_call(ya, yb, yc, yd, p, h, mod, w_out[l].astype(BF16), hgrn_norm_w[l], ssm_norm_w[l],
                          ln1_g[l], ln1_b[l], nb)
        wf = ffn_w_in[l]
        h = _ffn_call(h, mod, wf[:, :D_FF].astype(BF16), wf[:, D_FF:].astype(BF16),
                      ffn_w_out[l].astype(BF16), ffn_conv_w[l], ffn_conv_b[l], ln2_g[l], ln2_b[l],
                      nb, skip_ctx=(l == depth - 1))
    return h
```

```python
import functools
import math

import jax
import jax.numpy as jnp
from jax import lax
from jax.experimental import pallas as pl
from jax.experimental.pallas import tpu as pltpu

F32 = jnp.float32
BF16 = jnp.bfloat16

D_MODEL = 1024
DEPTH = 4
GRID_W = 64
CTX_LEN = 256
GROUP_W = 512
A_HEAD = 128
A_HEADS = 4
LB_FLOOR = 1e-30
B_HEADDIM = 64
B_HEADS = 8
B_GROUPS = 2
B_STATE = 128
XBC_W = GROUP_W + 2 * B_GROUPS * B_STATE
C_BLOCKS = 8
C_BLOCK = 64
C_POW = 8.0
D_HEAD = 128
D_HEADS = 4
ROPE_BASE = 10000.0
D_FF = 2816
ALPHA = (2 * DEPTH) ** 0.25
EPS = 1e-6
LOG2E = 1.4426950408889634

TM = 256
FFN_HALO = 8
A_CHUNK = 64
A_SUB = 8
A_FBLK = 16
A_GUARD = 80.0
B_CHUNK = 128
D_CHUNK = 128
FF_CHUNK = 256
FFN_GBUF = 4
VMEM_LIMIT = 56 * 1024 * 1024

COL_AQ, COL_AI, COL_AF, COL_AG = 0, 512, 1024, 2048
COL_BZ, COL_BXBC = 2560, 3072
COL_CX, COL_CG = 4096, 4608
COL_DQ, COL_DK, COL_DV, COL_DG = 5120, 5632, 6144, 6656
COL_DT = 7168
P_W = 7296
DT_W = 128
CONV_LO, CONV_HI = COL_BXBC, COL_CG


def _cparams(sem):
    return pltpu.CompilerParams(dimension_semantics=sem, vmem_limit_bytes=VMEM_LIMIT)


def _softplus(x):
    return jnp.maximum(x, 0.0) + jnp.log(1.0 + jnp.exp(-jnp.abs(x)))


def _sigmoid(x):
    return 1.0 / (1.0 + jnp.exp(-x))


def _silu(x):
    return x * _sigmoid(x)


def _gelu_tanh(x):
    return 0.5 * x * (1.0 + jnp.tanh(math.sqrt(2.0 / math.pi) * (x + 0.044715 * (x * x * x))))


def _dot(a, b):
    return jnp.dot(a, b, preferred_element_type=F32)


def _dot_nt(a, b):
    return lax.dot_general(a, b, (((1,), (1,)), ((), ())), preferred_element_type=F32)


def _split3(x):
    hi = x.astype(BF16)
    r1 = x - hi.astype(F32)
    mid = r1.astype(BF16)
    lo = (r1 - mid.astype(F32)).astype(BF16)
    return hi, mid, lo


def _sel_dot(m01, x):
    hi, mid, lo = _split3(x)
    return _dot(m01, hi) + _dot(m01, mid) + _dot(m01, lo)


def _seq_block(d, j, nblk):
    return jnp.where(d == 0, j, jnp.where(j == 0, 0, nblk - j))


def _ada_kernel(c_ref, w_ref, b_ref, o_ref):
    s = _silu(c_ref[...])
    o_ref[0] = jnp.dot(s, w_ref[0], preferred_element_type=F32,
                       precision=lax.Precision.HIGHEST) + b_ref[0]


def _ada_call(cond, ada_w, ada_b):
    depth, d, n6 = ada_w.shape
    rows = cond.shape[0]
    tn = 1536
    return pl.pallas_call(
        _ada_kernel,
        grid=(depth, n6 // tn),
        in_specs=[pl.BlockSpec((rows, d), lambda l, n: (0, 0)),
                  pl.BlockSpec((1, d, tn), lambda l, n: (l, 0, n)),
                  pl.BlockSpec((1, 1, tn), lambda l, n: (l, 0, n))],
        out_specs=pl.BlockSpec((1, rows, tn), lambda l, n: (l, 0, n)),
        out_shape=jax.ShapeDtypeStruct((depth, rows, n6), F32),
        compiler_params=_cparams(("parallel", "parallel")),
        name="ada_mod",
    )(cond, ada_w, ada_b.reshape(depth, 1, n6))


def _inproj_kernel(h_ref, hp_ref, hn_ref, mod_ref, w_ref, cw_ref, cb_ref, o_ref, g_ref, *, nb, tn, nblk):
    rb = pl.program_id(1)
    prev_ok = rb >= 2
    next_ok = (rb >= 1) & (rb <= nblk - 2)
    m = mod_ref[:, 0]
    sc, sh = 1.0 + m[:, 1:2, :], m[:, 0:1, :]
    hrows = TM + 2 * FFN_HALO
    ub = (h_ref[...] * sc + sh).astype(BF16).reshape(nb * TM, D_MODEL)
    for n0 in list(range(0, CONV_LO, tn)) + list(range(CONV_HI, P_W, tn)):
        w = min(tn, P_W - n0)
        o_ref[:, :, n0:n0 + w] = _dot(ub, w_ref[:, n0:n0 + w]).astype(BF16).reshape(nb, TM, w)
    u_all = (jnp.concatenate([hp_ref[...], h_ref[...], hn_ref[...]], axis=1) * sc + sh).astype(BF16)
    u_all = u_all.reshape(nb * hrows, D_MODEL)
    r1 = lax.broadcasted_iota(jnp.int32, (1, hrows, 1), 1)
    keep = ((r1 >= FFN_HALO) | prev_ok) & ((r1 < FFN_HALO + TM) | next_ok)
    for n0 in range(CONV_LO, CONV_HI, tn):
        cs = slice(n0 - CONV_LO, n0 - CONV_LO + tn)
        g = _dot(u_all, w_ref[:, n0:n0 + tn]).reshape(nb, hrows, tn)
        g_ref[...] = jnp.where(keep, g, 0.0)
        acc = cb_ref[:, cs].reshape(1, 1, tn)
        for jj in range(4):
            acc = acc + cw_ref[jj:jj + 1, cs].reshape(1, 1, tn) * g_ref[:, pl.ds(FFN_HALO - 2 + jj, TM), :]
        if n0 < COL_CX:
            acc = _silu(acc)
        o_ref[:, :, n0:n0 + tn] = acc.astype(BF16)


def _inproj_call(h, mod, w_in_b, conv_w, conv_b, nb):
    b, l, d = h.shape
    nblk = l // TM
    per = TM // FFN_HALO
    last = l // FFN_HALO - 1
    ncv = CONV_HI - CONV_LO
    kern = functools.partial(_inproj_kernel, nb=nb, tn=512, nblk=nblk)
    return pl.pallas_call(
        kern,
        grid=(b // nb, nblk),
        in_specs=[pl.BlockSpec((nb, TM, d), lambda i, t: (i, t, 0)),
                  pl.BlockSpec((nb, FFN_HALO, d), lambda i, t: (i, jnp.maximum(t * per - 1, 0), 0)),
                  pl.BlockSpec((nb, FFN_HALO, d), lambda i, t: (i, jnp.minimum((t + 1) * per, last), 0)),
                  pl.BlockSpec((nb, 1, 6, d), lambda i, t: (i, jnp.minimum(t, 1), 0, 0)),
                  pl.BlockSpec((d, P_W), lambda i, t: (0, 0), pipeline_mode=pl.Buffered(1)),
                  pl.BlockSpec((4, ncv), lambda i, t: (0, 0)),
                  pl.BlockSpec((1, ncv), lambda i, t: (0, 0))],
        out_specs=pl.BlockSpec((nb, TM, P_W), lambda i, t: (i, t, 0)),
        out_shape=jax.ShapeDtypeStruct((b, l, P_W), BF16),
        scratch_shapes=[pltpu.VMEM((nb, TM + 2 * FFN_HALO, 512), F32)],
        compiler_params=_cparams(("parallel", "parallel")),
        name="in_proj",
    )(h, h, h, mod, w_in_b, conv_w, conv_b)


def _mixa_body(q_ref, v_ref, f_ref, lb_ref, y_ref, st_ref, sh_ref, rev):
    C, m = A_CHUNK, A_SUB
    nsl = C // m - 1
    lbv = lb_ref[0]
    log_lb = jnp.log(jnp.maximum(lbv, LB_FLOOR))
    log_1mlb = jnp.log1p(-lbv)
    one_m_lb = 1.0 - lbv

    row = lax.broadcasted_iota(jnp.int32, (C, C), 0)
    col = lax.broadcasted_iota(jnp.int32, (C, C), 1)
    tau_r = (C - 1 - row) if rev else row
    tau_c = (C - 1 - col) if rev else col
    blk_r, blk_c = tau_r // m, tau_c // m
    tri = (tau_c <= tau_r).astype(BF16)
    refrow = (C - m * blk_r) if rev else (m * blk_r - 1)
    rsel = ((col == refrow) & (blk_r >= 1)).astype(BF16)
    off_mask = blk_c < blk_r
    r1 = lax.broadcasted_iota(jnp.int32, (C, 1), 0)
    tau1 = (C - 1 - r1) if rev else r1
    blk1 = tau1 // m
    off1 = tau1 % m

    pad = m
    for c in (range(TM // C - 1, -1, -1) if rev else range(TM // C)):
        sl = pl.ds(c * C, C)
        q = q_ref[0, sl, :].astype(F32)
        v = v_ref[0, sl, :].astype(F32)
        u = f_ref[0, sl, :].astype(F32)
        log_sig = jnp.minimum(u, 0.0) - jnp.log1p(jnp.exp(-jnp.abs(u)))
        t2 = log_1mlb + log_sig
        log_f = jnp.maximum(log_lb, t2) + jnp.log1p(jnp.exp(-jnp.abs(log_lb - t2)))
        k = one_m_lb * _sigmoid(-u)
        b = _sel_dot(tri, log_f)
        rr = _sel_dot(rsel, b)
        qt = q * jnp.exp(jnp.minimum(b - rr, 0.0))
        sh_ref[0, pl.ds(pad, C), :] = k
        sh_ref[1, pl.ds(pad, C), :] = b
        sh_ref[2, pl.ds(pad, C), :] = v
        y_parts = []
        for h in range(A_HEADS):
            hs = slice(h * A_HEAD, (h + 1) * A_HEAD)
            qh, kh, vh, bh, qth = q[:, hs], k[:, hs], v[:, hs], b[:, hs], qt[:, hs]
            tot = bh[0:1, :] if rev else bh[C - 1:C, :]
            q_aug = jnp.concatenate(
                [jnp.where(blk1 == i, qth, 0.0) for i in range(1, nsl + 1)], axis=1).astype(BF16)
            k_parts = []
            for i in range(1, nsl + 1):
                rrow = (C - m * i) if rev else (m * i - 1)
                k_parts.append(kh * jnp.exp(jnp.minimum(bh[rrow:rrow + 1, :] - bh, 0.0)))
            k_aug = jnp.concatenate(k_parts, axis=1).astype(BF16)
            p_off = jnp.where(off_mask, _dot_nt(q_aug, k_aug), 0.0)
            y = _dot(p_off.astype(BF16), vh.astype(BF16))
            for dl in range(m):
                if dl == 0:
                    ks, bs, vs = kh, bh, vh
                else:
                    st = pad + dl if rev else pad - dl
                    ks = sh_ref[0, pl.ds(st, C), hs]
                    bs = sh_ref[1, pl.ds(st, C), hs]
                    vs = sh_ref[2, pl.ds(st, C), hs]
                tmp = qh * ks * jnp.exp(jnp.minimum(bh - bs, 0.0))
                p = jnp.sum(tmp, axis=-1, keepdims=True)
                y = y + jnp.where(off1 >= dl, p, 0.0) * vs
            st_t = st_ref[h]
            y = y + _dot_nt((qh * jnp.exp(bh)).astype(BF16), st_t.astype(BF16))
            kdec = (kh * jnp.exp(tot - bh)).astype(BF16)
            st_ref[h] = st_t * jnp.exp(tot) + _dot(vh.T.astype(BF16), kdec)
            y_parts.append(y)
        y_ref[0, 0, sl, :] = jnp.concatenate(y_parts, axis=1).astype(BF16)


def _mixa_tables(tm_ref, code_ref, rev):
    C, m = A_CHUNK, A_FBLK
    row = lax.broadcasted_iota(jnp.int32, (TM, TM), 0)
    col = lax.broadcasted_iota(jnp.int32, (TM, TM), 1)
    same_chunk = (row // C) == (col // C)
    tr, tc = row % C, col % C
    if rev:
        tr, tc = C - 1 - tr, C - 1 - tc
    same_blk = same_chunk & ((tr // m) == (tc // m))
    orr, oc = tr % m, tc % m
    half = m // 2
    tm_ref[0] = (same_chunk & (tc <= tr)).astype(BF16)
    tm_ref[1] = (same_blk & (oc <= orr)).astype(BF16)
    plus = same_blk & (oc >= half) & (oc <= orr)
    minus = same_blk & (oc > orr) & (oc < half)
    tm_ref[2] = (plus.astype(F32) - minus.astype(F32)).astype(BF16)
    off = same_chunk & ((tc // m) < (tr // m))
    diag = same_blk & (tc <= tr)
    code_ref[...] = jnp.where(off, 1.0, jnp.where(diag, 2.0, 0.0))


def _mixa_fast(q_ref, v_ref, y_ref, st_ref, code_ref, k_ref, b_ref, bl_ref, bm_ref, rev):
    C, m = A_CHUNK, A_FBLK
    nch = TM // C
    nsl = C // m - 1
    code = code_ref[...]
    off_mask = code == 1.0
    diag_mask = code == 2.0
    r1 = lax.broadcasted_iota(jnp.int32, (TM, 1), 0) % C
    blk1 = ((C - 1 - r1) if rev else r1) // m
    rows = lambda a, c: a[c * C:(c + 1) * C, :]
    order = range(nch - 1, -1, -1) if rev else range(nch)
    y_parts = []
    for h in range(A_HEADS):
        hs = slice(h * A_HEAD, (h + 1) * A_HEAD)
        qh = q_ref[0, :, hs].astype(F32)
        vb = v_ref[0, :, hs]
        kh, bh, blh, bmh = k_ref[:, hs], b_ref[:, hs], bl_ref[:, hs], bm_ref[:, hs]
        qt = qh * jnp.exp2(blh)
        q_aug = jnp.concatenate(
            [jnp.where(blk1 == i, qt, 0.0) for i in range(1, nsl + 1)], axis=1).astype(BF16)
        k_parts = []
        for i in range(1, nsl + 1):
            rrow = (C - m * i) if rev else (m * i - 1)
            dif = jnp.concatenate([bh[c * C + rrow:c * C + rrow + 1, :] - rows(bh, c) for c in range(nch)], axis=0)
            k_parts.append(kh * jnp.exp2(jnp.minimum(dif, 0.0)))
        k_aug = jnp.concatenate(k_parts, axis=1).astype(BF16)
        p = jnp.where(off_mask, _dot_nt(q_aug, k_aug), 0.0)
        qd = (qh * jnp.exp2(bmh)).astype(BF16)
        kd = (kh * jnp.exp2(-bmh)).astype(BF16)
        p = p + jnp.where(diag_mask, _dot_nt(qd, kd), 0.0)
        y = _dot(p.astype(BF16), vb)
        qe = (qh * jnp.exp2(bh)).astype(BF16)
        tots = [bh[c * C:c * C + 1, :] if rev else bh[(c + 1) * C - 1:(c + 1) * C, :] for c in range(nch)]
        upd = [_dot(rows(vb, c).astype(F32).T.astype(BF16),
                    (rows(kh, c) * jnp.exp2(tots[c] - rows(bh, c))).astype(BF16)) for c in range(nch)]
        st = st_ref[h]
        y_inter = [None] * nch
        for c in order:
            y_inter[c] = _dot_nt(rows(qe, c), st.astype(BF16))
            st = st * jnp.exp2(tots[c]) + upd[c]
        st_ref[h] = st
        y_parts.append(y + jnp.concatenate(y_inter, axis=0))
    y_ref[0, 0] = jnp.concatenate(y_parts, axis=1).astype(BF16)


def _mixa_prepare(f_ref, lb_ref, tm_ref, k_ref, b_ref, bl_ref, bm_ref):
    lbv = lb_ref[0]
    lb_floor = jnp.maximum(lbv, LB_FLOOR)
    one_m_lb = 1.0 - lbv
    worst = jnp.zeros((1, 1), F32)
    lw = 2 * A_HEAD
    for h in range(GROUP_W // lw):
        hs = slice(h * lw, (h + 1) * lw)
        u = f_ref[0, :, hs].astype(F32)
        e = jnp.exp(-jnp.abs(u))
        big = 1.0 / (1.0 + e)
        small = e * big
        pos = u >= 0.0
        sig_u = jnp.where(pos, big, small)
        sig_mu = jnp.where(pos, small, big)
        k_ref[:, hs] = one_m_lb[:, hs] * sig_mu
        lf2 = jnp.log2(lb_floor[:, hs] + one_m_lb[:, hs] * sig_u)
        hi = lf2.astype(BF16)
        lo = (lf2 - hi.astype(F32)).astype(BF16)
        cum = lambda i: _dot(tm_ref[i], hi) + _dot(tm_ref[i], lo)
        b_ref[:, hs] = cum(0)
        bl_ref[:, hs] = cum(1)
        bm = cum(2)
        bm_ref[:, hs] = bm
        worst = jnp.maximum(worst, jnp.max(jnp.abs(bm), keepdims=True))
    return worst[0, 0] <= A_GUARD * LOG2E


def _mixb_body(xc_ref, dt_ref, dtb_ref, alog_ref, e_ref, dsk_ref, y_ref, st_ref, tri_ref, rev, d):
    C = B_CHUNK
    hpg = B_HEADS // B_GROUPS
    gw = hpg * B_HEADDIM
    row = lax.broadcasted_iota(jnp.int32, (C, C), 0)
    col = lax.broadcasted_iota(jnp.int32, (C, C), 1)
    trimask = (col >= row) if rev else (col <= row)
    tri = tri_ref[...]
    lane_head = lax.broadcasted_iota(jnp.int32, (1, gw), 1) // B_HEADDIM
    e01 = e_ref[0]
    dtb = dtb_ref[0]
    neg_a = -jnp.exp(alog_ref[0])

    def expand(x):
        hi, mid, lo = _split3(x)
        return _dot(hi, e01) + _dot(mid, e01) + _dot(lo, e01)

    for c in (range(TM // C - 1, -1, -1) if rev else range(TM // C)):
        sl = pl.ds(c * C, C)
        bx = xc_ref[0, sl, 0:GROUP_W].astype(F32)
        bb = xc_ref[0, sl, GROUP_W:GROUP_W + B_GROUPS * B_STATE]
        bc = xc_ref[0, sl, GROUP_W + B_GROUPS * B_STATE:XBC_W]
        dt = _softplus(dt_ref[0, sl, :].astype(F32) + dtb)
        la = dt * neg_a
        b = _sel_dot(tri, la)
        b_exp = expand(b)
        dt_exp = expand(dt)
        tot_exp = b_exp[0:1, :] if rev else b_exp[C - 1:C, :]
        v_all = bx * dt_exp
        vdec = v_all * jnp.exp(tot_exp - b_exp)
        eb = jnp.exp(b_exp)
        bt = b.T
        y_parts = []
        for g in range(B_GROUPS):
            gs = slice(g * gw, (g + 1) * gw)
            bg = bb[:, g * B_STATE:(g + 1) * B_STATE]
            cg = bc[:, g * B_STATE:(g + 1) * B_STATE]
            qk = _dot_nt(cg, bg)
            s_g = st_ref[g]
            vg = v_all[:, gs].astype(BF16)
            m_parts, v_parts = [], []
            for e in range(hpg):
                lane = 8 * d + g * hpg + e
                rel = jnp.where(trimask,
                                jnp.exp(jnp.minimum(b[:, lane:lane + 1] - bt[lane:lane + 1, :], 0.0)), 0.0)
                m_parts.append((qk * rel).astype(BF16))
                v_parts.append(jnp.where(lane_head == e, vg, jnp.zeros_like(vg)))
            y_g = (_dot(jnp.concatenate(m_parts, axis=1), jnp.concatenate(v_parts, axis=0))
                   + _dot(cg, s_g.astype(BF16)) * eb[:, gs])
            st_ref[g] = (s_g * jnp.exp(tot_exp[:, gs])
                         + _dot(bg.astype(F32).T.astype(BF16), vdec[:, gs].astype(BF16)))
            y_parts.append(y_g)
        y = jnp.concatenate(y_parts, axis=1)
        if not rev:
            y = y + dsk_ref[...] * bx
        y_ref[0, 0, sl, :] = y.astype(BF16)


def _mixc_body(y_ref, carry_ref, a_ref, b_ref, rev):
    grp = TM // 8
    a3 = a_ref[...].reshape(grp, 8, GROUP_W)
    b3 = b_ref[...].reshape(grp, 8, GROUP_W)
    ro = lax.broadcasted_iota(jnp.int32, (1, 8, 1), 1)
    for s in (1, 2, 4):
        valid = (ro + s <= 7) if rev else (ro >= s)
        shift = (8 - s) if rev else s
        a_sh = pltpu.roll(a3, shift, 1)
        b_sh = pltpu.roll(b3, shift, 1)
        b3 = b3 + a3 * jnp.where(valid, b_sh, 0.0)
        a3 = a3 * jnp.where(valid, a_sh, 1.0)
    cb = jnp.broadcast_to(carry_ref[...], (8, GROUP_W))
    last = 0 if rev else 7
    hs = [None] * grp
    for g in (range(grp - 1, -1, -1) if rev else range(grp)):
        hg = a3[g] * cb + b3[g]
        hs[g] = hg
        cb = jnp.broadcast_to(hg[last:last + 1, :], (8, GROUP_W))
    carry_ref[...] = cb[0:1, :]
    y_ref[0, 0] = jnp.concatenate(hs, axis=0).astype(BF16)


def _mixc_gates(x_ref, wa_ref, wi_ref, ba_ref, bi_ref, lam_ref, a_ref, b_ref):
    cxb = x_ref[0]
    cx = cxb.astype(F32)
    r = _sigmoid(_dot(cxb, wa_ref[0]) + ba_ref[0])
    gi = _sigmoid(_dot(cxb, wi_ref[0]) + bi_ref[0])
    la = (-C_POW) * r * _softplus(-lam_ref[0])
    a = jnp.exp(la)
    a_ref[...] = a
    b_ref[...] = jnp.sqrt(1.0 - a * a) * gi * cx


def _block_diag(w):
    nd, nb_, n, _ = w.shape
    eye = jnp.eye(nb_, dtype=w.dtype)
    return jnp.einsum("dhij,hg->dhigj", w, eye).reshape(nd, nb_ * n, nb_ * n)


def _mixd_body(q_ref, k_ref, v_ref, cos_ref, sin_ref, lgt_ref, y_ref, st_ref, rev):
    C = D_CHUNK
    lg = -_softplus(-lgt_ref[0])
    row = lax.broadcasted_iota(jnp.int32, (C, C), 0)
    col = lax.broadcasted_iota(jnp.int32, (C, C), 1)
    trimask = (col >= row) if rev else (col <= row)
    dist = jnp.maximum((col - row) if rev else (row - col), 0).astype(F32)
    r1 = lax.broadcasted_iota(jnp.int32, (C, 1), 0)
    tau1 = ((C - 1 - r1) if rev else r1).astype(F32)
    scale = D_HEAD ** -0.5
    for c in (range(TM // C - 1, -1, -1) if rev else range(TM // C)):
        sl = pl.ds(c * C, C)
        cosf = cos_ref[sl, :]
        sinf = sin_ref[sl, :]
        y_parts = []
        for h in range(D_HEADS):
            hs = slice(h * D_HEAD, (h + 1) * D_HEAD)
            lgh = lg[h:h + 1, 0:1]
            qh = q_ref[0, sl, hs].astype(F32)
            kh = k_ref[0, sl, hs].astype(F32)
            vh = v_ref[0, sl, hs]
            qr = qh * cosf + pltpu.roll(qh, D_HEAD // 2, 1) * sinf
            kr = (kh * cosf + pltpu.roll(kh, D_HEAD // 2, 1) * sinf) * scale
            qb = qr.astype(BF16)
            qk = _dot_nt(qb, kr.astype(BF16))
            rel = jnp.where(trimask, jnp.exp(lgh * dist), 0.0)
            s_h = st_ref[h]
            y = _dot((qk * rel).astype(BF16), vh) + _dot(qb, s_h.astype(BF16)) * jnp.exp(lgh * (tau1 + 1.0))
            kdec = kr * jnp.exp(lgh * (C - 1.0 - tau1))
            st_ref[h] = s_h * jnp.exp(lgh * float(C)) + _dot(kdec.T.astype(BF16), vh)
            y_parts.append(y)
        y_ref[0, 0, sl, :] = jnp.concatenate(y_parts, axis=1).astype(BF16)


def _mix_kernel(aq_ref, ai_ref, af_ref, lb_ref,
                bx_ref, bdt_ref, dtb_ref, alog_ref, e_ref, dsk_ref,
                cx_ref, wa_ref, wi_ref, ba_ref, bi_ref, lam_ref,
                dq_ref, dk_ref, dv_ref, cos_ref, sin_ref, lgt_ref,
                ya_ref, yb_ref, yc_ref, yd_ref,
                a_st, a_sh, a_tm, a_code, a_k, a_b, a_bl, a_bm, b_st, c_carry, c_a, c_b, d_st, a_bak, b_tri):
    d = pl.program_id(1)
    j = pl.program_id(2)

    @pl.when(j == 0)
    def _():
        for r in (a_st, a_sh, b_st, c_carry, d_st):
            r[...] = jnp.zeros_like(r)

    def run(rev, dd):
        @pl.when(j == 0)
        def _():
            _mixa_tables(a_tm, a_code, rev)
            row = lax.broadcasted_iota(jnp.int32, (B_CHUNK, B_CHUNK), 0)
            col = lax.broadcasted_iota(jnp.int32, (B_CHUNK, B_CHUNK), 1)
            b_tri[...] = ((col >= row) if rev else (col <= row)).astype(BF16)

        ok = _mixa_prepare(af_ref, lb_ref, a_tm, a_k, a_b, a_bl, a_bm)
        a_bak[...] = a_st[...]
        _mixa_fast(aq_ref, ai_ref, ya_ref, a_st, a_code, a_k, a_b, a_bl, a_bm, rev)
        _mixb_body(bx_ref, bdt_ref, dtb_ref, alog_ref, e_ref, dsk_ref, yb_ref, b_st, b_tri, rev, dd)
        _mixc_gates(cx_ref, wa_ref, wi_ref, ba_ref, bi_ref, lam_ref, c_a, c_b)
        _mixc_body(yc_ref, c_carry, c_a, c_b, rev)
        _mixd_body(dq_ref, dk_ref, dv_ref, cos_ref, sin_ref, lgt_ref, yd_ref, d_st, rev)

        @pl.when(jnp.logical_not(ok))
        def _():
            a_st[...] = a_bak[...]
            _mixa_body(aq_ref, ai_ref, af_ref, lb_ref, ya_ref, a_st, a_sh, rev)

    @pl.when(d == 0)
    def _():
        run(False, 0)

    @pl.when(d == 1)
    def _():
        run(True, 1)


def _mix_call(p, lb, dt_bias, a_log, ssm_d, wa, ba, wi, bi, lam, cosf, sinf, decay_logit):
    b, l, _ = p.shape
    nblk = l // TM
    rowblk = lambda d, j: _seq_block(d, j, nblk)
    pcol = lambda col, width=GROUP_W: pl.BlockSpec((1, TM, width), lambda i, d, j: (i, rowblk(d, j), col // width))
    dirvec = lambda width: pl.BlockSpec((1, 1, width), lambda i, d, j: (d, 0, 0))
    pad_dir = lambda a: jnp.stack([jnp.pad(a[0], (0, DT_W - B_HEADS)),
                                   jnp.pad(a[1], (B_HEADS, DT_W - 2 * B_HEADS))]).reshape(2, 1, DT_W)
    lanes = jnp.arange(GROUP_W) // B_HEADDIM
    e01 = jnp.stack([(jnp.arange(DT_W)[:, None] == lanes[None, :] + 8 * dd) for dd in range(2)]).astype(BF16)
    dsk = jnp.repeat(ssm_d, B_HEADDIM).reshape(1, GROUP_W)
    vec = lambda a: a.reshape(2, 1, GROUP_W)
    wspec = pl.BlockSpec((1, GROUP_W, GROUP_W), lambda i, d, j: (d, 0, 0))
    lgt = jnp.broadcast_to(jnp.pad(decay_logit, ((0, 0), (0, 8 - D_HEADS)))[:, :, None], (2, 8, D_HEAD))
    tab = pl.BlockSpec((TM, D_HEAD), lambda i, d, j: (rowblk(d, j), 0))
    yspec = pl.BlockSpec((1, 1, TM, GROUP_W), lambda i, d, j: (d, i, rowblk(d, j), 0))
    yshape = jax.ShapeDtypeStruct((2, b, l, GROUP_W), BF16)
    act = lambda: pltpu.VMEM((TM, GROUP_W), F32)
    return pl.pallas_call(
        _mix_kernel,
        grid=(b, 2, nblk),
        in_specs=[pcol(COL_AQ), pcol(COL_AI),
                  pl.BlockSpec((1, TM, GROUP_W), lambda i, d, j: (i, rowblk(d, j), COL_AF // GROUP_W + d)),
                  dirvec(GROUP_W),
                  pcol(COL_BXBC, XBC_W), pcol(COL_DT, DT_W), dirvec(DT_W), dirvec(DT_W),
                  pl.BlockSpec((1, DT_W, GROUP_W), lambda i, d, j: (d, 0, 0)),
                  pl.BlockSpec((1, GROUP_W), lambda i, d, j: (0, 0)),
                  pcol(COL_CX), wspec, wspec, dirvec(GROUP_W), dirvec(GROUP_W), dirvec(GROUP_W),
                  pcol(COL_DQ), pcol(COL_DK), pcol(COL_DV), tab, tab,
                  pl.BlockSpec((1, 8, D_HEAD), lambda i, d, j: (d, 0, 0))],
        out_specs=[yspec, yspec, yspec, yspec],
        out_shape=[yshape, yshape, yshape, yshape],
        scratch_shapes=[pltpu.VMEM((A_HEADS, A_HEAD, A_HEAD), F32),
                        pltpu.VMEM((3, A_CHUNK + 2 * A_SUB, GROUP_W), F32),
                        pltpu.VMEM((3, TM, TM), BF16),
                        pltpu.VMEM((TM, TM), F32),
                        act(), act(), act(), act(),
                        pltpu.VMEM((B_GROUPS, B_STATE, GROUP_W // B_GROUPS), F32),
                        pltpu.VMEM((1, GROUP_W), F32), act(), act(),
                        pltpu.VMEM((D_HEADS, D_HEAD, D_HEAD), F32),
                        pltpu.VMEM((A_HEADS, A_HEAD, A_HEAD), F32),
                        pltpu.VMEM((B_CHUNK, B_CHUNK), BF16)],
        compiler_params=_cparams(("parallel", "parallel", "arbitrary")),
        name="mixers",
    )(p, p, p, lb.reshape(2, 1, GROUP_W),
      p, p, pad_dir(dt_bias), pad_dir(a_log), e01, dsk,
      p, _block_diag(wa).astype(BF16), _block_diag(wi).astype(BF16), vec(ba), vec(bi), vec(lam),
      p, p, p, cosf, sinf, lgt)


def _layer_norm_rows(x, g, b):
    mu = jnp.mean(x, axis=-1, keepdims=True)
    xc = x - mu
    var = jnp.mean(xc * xc, axis=-1, keepdims=True)
    return xc * lax.rsqrt(var + EPS) * g + b


def _outproj_kernel(yaf, yab, ybf, ybb, ycf, ycb, ydf, ydb, ag_ref, bz_ref, cg_ref, dg_ref,
                    h_ref, mod_ref, w_ref, nwa_ref, nwb_ref, lg_ref, lb_ref, o_ref, *, nb):
    rows = nb * TM
    ld = lambda f, bk: (f[0].astype(F32) + bk[0].astype(F32)).reshape(rows, GROUP_W)
    gate = lambda r: r[...].astype(F32).reshape(rows, GROUP_W)
    ya = ld(yaf, yab)
    parts = []
    for h in range(A_HEADS):
        x = ya[:, h * A_HEAD:(h + 1) * A_HEAD]
        parts.append(x * lax.rsqrt(jnp.mean(x * x, axis=-1, keepdims=True) + EPS))
    oa = jnp.concatenate(parts, axis=1) * nwa_ref[...] * _silu(gate(ag_ref))
    yb = ld(ybf, ybb) * _silu(gate(bz_ref))
    gw = GROUP_W // B_GROUPS
    parts = []
    for g in range(B_GROUPS):
        x = yb[:, g * gw:(g + 1) * gw]
        parts.append(x * lax.rsqrt(jnp.mean(x * x, axis=-1, keepdims=True) + EPS))
    ob = jnp.concatenate(parts, axis=1) * nwb_ref[...]
    oc = ld(ycf, ycb) * _gelu_tanh(gate(cg_ref))
    yd = ld(ydf, ydb)
    parts = []
    for h in range(D_HEADS):
        x = yd[:, h * D_HEAD:(h + 1) * D_HEAD]
        mu = jnp.mean(x, axis=-1, keepdims=True)
        xc = x - mu
        parts.append(xc * lax.rsqrt(jnp.mean(xc * xc, axis=-1, keepdims=True) + EPS))
    od = jnp.concatenate(parts, axis=1) * _silu(gate(dg_ref))
    mix = (_dot(oa.astype(BF16), w_ref[0:GROUP_W, :])
           + _dot(ob.astype(BF16), w_ref[GROUP_W:2 * GROUP_W, :])
           + _dot(oc.astype(BF16), w_ref[2 * GROUP_W:3 * GROUP_W, :])
           + _dot(od.astype(BF16), w_ref[3 * GROUP_W:4 * GROUP_W, :]))
    m = mod_ref[:, 0]
    hn = ALPHA * h_ref[...] + m[:, 2:3, :] * mix.reshape(nb, TM, D_MODEL)
    o_ref[...] = _layer_norm_rows(hn, lg_ref[...], lb_ref[...])


def _outproj_call(ya, yb, yc, yd, p, h, mod, w_out_b, nwa, nwb, ln_g, ln_b, nb):
    b, l, d = h.shape
    ydir = lambda dd: pl.BlockSpec((1, nb, TM, GROUP_W), lambda i, t: (dd, i, t, 0))
    pcol = lambda col: pl.BlockSpec((nb, TM, GROUP_W), lambda i, t: (i, t, col // GROUP_W))
    vec = lambda n: pl.BlockSpec((1, n), lambda i, t: (0, 0))
    return pl.pallas_call(
        functools.partial(_outproj_kernel, nb=nb),
        grid=(b // nb, l // TM),
        in_specs=[ydir(0), ydir(1), ydir(0), ydir(1), ydir(0), ydir(1), ydir(0), ydir(1),
                  pcol(COL_AG), pcol(COL_BZ), pcol(COL_CG), pcol(COL_DG),
                  pl.BlockSpec((nb, TM, d), lambda i, t: (i, t, 0)),
                  pl.BlockSpec((nb, 1, 6, d), lambda i, t: (i, jnp.minimum(t, 1), 0, 0)),
                  pl.BlockSpec((4 * GROUP_W, d), lambda i, t: (0, 0), pipeline_mode=pl.Buffered(1)),
                  vec(GROUP_W), vec(GROUP_W), vec(d), vec(d)],
        out_specs=pl.BlockSpec((nb, TM, d), lambda i, t: (i, t, 0)),
        out_shape=jax.ShapeDtypeStruct((b, l, d), F32),
        compiler_params=_cparams(("parallel", "parallel")),
        name="out_proj",
    )(ya, ya, yb, yb, yc, yc, yd, yd, p, p, p, p, h, mod, w_out_b,
      nwa.reshape(1, GROUP_W), nwb.reshape(1, GROUP_W), ln_g.reshape(1, d), ln_b.reshape(1, d))


def _ffn_kernel(h_ref, hp_ref, hn_ref, mod_ref, wg_ref, wu_ref, wo_ref, cw_ref, cb_ref, lg_ref, lb_ref,
                o_ref, act_ref, g_ref, *, nb, nblk, first_blk):
    rb = pl.program_id(1) + first_blk
    prev_ok = rb >= 2
    next_ok = (rb >= 1) & (rb <= nblk - 2)
    m = mod_ref[:, 0]
    sh, sc = m[:, 3:4, :], m[:, 4:5, :]
    hm = h_ref[...]
    hrows = TM + 2 * FFN_HALO
    x_all = jnp.concatenate([hp_ref[...], hm, hn_ref[...]], axis=1) * (1.0 + sc) + sh
    xb_all = x_all.astype(BF16).reshape(nb * hrows, D_MODEL)
    xb = x_all[:, FFN_HALO:FFN_HALO + TM, :].astype(BF16).reshape(nb * TM, D_MODEL)
    r1 = lax.broadcasted_iota(jnp.int32, (1, hrows, 1), 1)
    keep = ((r1 >= FFN_HALO) | prev_ok) & ((r1 < FFN_HALO + TM) | next_ok)
    for c0 in range(0, D_FF, FF_CHUNK):
        cs = slice(c0, c0 + FF_CHUNK)
        g = _dot(xb_all, wg_ref[:, cs]).reshape(nb, hrows, FF_CHUNK)
        gi = (c0 // FF_CHUNK) % FFN_GBUF
        g_ref[gi] = jnp.where(keep, g, 0.0)
        gc = cb_ref[:, cs].reshape(1, 1, FF_CHUNK)
        for jj in range(3):
            gc = gc + (cw_ref[jj:jj + 1, cs].reshape(1, 1, FF_CHUNK)
                       * g_ref[gi, :, pl.ds(FFN_HALO - 1 + jj, TM), :])
        up = _dot(xb, wu_ref[:, cs])
        act_ref[:, cs] = (_gelu_tanh(gc).reshape(nb * TM, FF_CHUNK) * up).astype(BF16)
    f = _dot(act_ref[...], wo_ref[...])
    hn2 = ALPHA * hm + m[:, 5:6, :] * f.reshape(nb, TM, D_MODEL)
    o_ref[...] = _layer_norm_rows(hn2, lg_ref[...], lb_ref[...])


def _ffn_call(h, mod, wg_b, wu_b, wo_b, conv_w, conv_b, ln_g, ln_b, nb, skip_ctx):
    b, l, d = h.shape
    nblk = l // TM
    first = 1 if skip_ctx else 0
    per = TM // FFN_HALO
    last = l // FFN_HALO - 1
    vec = lambda n: pl.BlockSpec((1, n), lambda i, t: (0, 0))
    res = lambda shape: pl.BlockSpec(shape, lambda i, t: (0, 0), pipeline_mode=pl.Buffered(1))
    return pl.pallas_call(
        functools.partial(_ffn_kernel, nb=nb, nblk=nblk, first_blk=first),
        grid=(b // nb, nblk - first),
        in_specs=[pl.BlockSpec((nb, TM, d), lambda i, t: (i, t + first, 0)),
                  pl.BlockSpec((nb, FFN_HALO, d), lambda i, t: (i, jnp.maximum((t + first) * per - 1, 0), 0)),
                  pl.BlockSpec((nb, FFN_HALO, d), lambda i, t: (i, jnp.minimum((t + first + 1) * per, last), 0)),
                  pl.BlockSpec((nb, 1, 6, d), lambda i, t: (i, jnp.minimum(t + first, 1), 0, 0)),
                  res((d, D_FF)), res((d, D_FF)), res((D_FF, d)),
                  pl.BlockSpec((3, D_FF), lambda i, t: (0, 0)), vec(D_FF), vec(d), vec(d)],
        out_specs=pl.BlockSpec((nb, TM, d), lambda i, t: (i, t, 0)),
        out_shape=jax.ShapeDtypeStruct((b, l - first * TM, d), F32),
        scratch_shapes=[pltpu.VMEM((nb * TM, D_FF), BF16),
                        pltpu.VMEM((FFN_GBUF, nb, TM + 2 * FFN_HALO, FF_CHUNK), F32)],
        compiler_params=_cparams(("parallel", "parallel")),
        name="conv_ffn",
    )(h, h, h, mod, wg_b, wu_b, wo_b, conv_w, conv_b.reshape(1, D_FF), ln_g.reshape(1, d), ln_b.reshape(1, d))


def _rope_tables(n):
    rows = n // GRID_W
    row = jnp.repeat(jnp.arange(rows), GRID_W)
    col = jnp.tile(jnp.arange(GRID_W), rows)
    n_freq = D_HEAD // 4
    inv = ROPE_BASE ** (-jnp.arange(n_freq, dtype=F32) / n_freq)
    ang = jnp.concatenate([row[:, None] * inv, col[:, None] * inv], axis=-1)
    ang = jnp.concatenate([jnp.zeros((CTX_LEN, D_HEAD // 2), F32), ang], axis=0)
    cos, sin = jnp.cos(ang), jnp.sin(ang)
    return jnp.concatenate([cos, cos], axis=-1), jnp.concatenate([-sin, sin], axis=-1)


def kernel(x, c, ctx, c_ctx, ada_w, ada_b, w_in, hgrn_lb_logits, hgrn_norm_w, ssm_conv_w, ssm_conv_b, ssm_dt_bias, ssm_a_log, ssm_d, ssm_norm_w, lru_conv_w, lru_conv_b, lru_wa, lru_ba, lru_wi, lru_bi, lru_lambda, ret_decay_logit, w_out, ln1_g, ln1_b, ffn_w_in, ffn_conv_w, ffn_conv_b, ffn_w_out, ln2_g, ln2_b):
    bsz, n, d = x.shape
    assert d == D_MODEL and ctx.shape[1] == CTX_LEN == TM and n % TM == 0 and n % GRID_W == 0
    nb = 2 if bsz % 2 == 0 else 1
    depth = ada_w.shape[0]

    cosf, sinf = _rope_tables(n)
    p_soft = jax.nn.softmax(hgrn_lb_logits.astype(F32), axis=0)
    lb_all = jnp.cumsum(p_soft, axis=0) - p_soft

    rows = -(-(bsz + 1) // 8) * 8
    cond = jnp.zeros((rows, d), F32).at[:bsz].set(c).at[bsz].set(c_ctx)
    mods = _ada_call(cond, ada_w, ada_b)

    n_in = w_in.shape[2]
    dt0 = COL_BXBC + XBC_W
    h = jnp.concatenate([ctx, x], axis=1)
    for l in range(depth):
        ml = mods[l].reshape(rows, 6, d)
        mod = jnp.stack([jnp.broadcast_to(ml[bsz][None], (bsz, 6, d)), ml[:bsz]], axis=1)
        w_l = w_in[l]
        w_perm = jnp.concatenate([w_l[:, :dt0], w_l[:, dt0 + 2 * B_HEADS:], w_l[:, dt0:dt0 + 2 * B_HEADS],
                                  jnp.zeros((d, P_W - n_in), F32)], axis=1).astype(BF16)
        conv_w = jnp.concatenate([ssm_conv_w[l], lru_conv_w[l]], axis=1)
        conv_b = jnp.concatenate([ssm_conv_b[l], lru_conv_b[l]]).reshape(1, CONV_HI - CONV_LO)
        p = _inproj_call(h, mod, w_perm, conv_w, conv_b, nb)
        ya, yb, yc, yd = _mix_call(p, lb_all[l], ssm_dt_bias[l], ssm_a_log[l], ssm_d[l],
                                   lru_wa[l], lru_ba[l], lru_wi[l], lru_bi[l], lru_lambda[l],
                                   cosf, sinf, ret_decay_logit[l].astype(F32))
        h = _outproj_call(ya, yb, yc, yd, p, h, mod, w_out[l].astype(BF16), hgrn_norm_w[l], ssm_norm_w[l],
                          ln1_g[l], ln1_b[l], nb)
        wf = ffn_w_in[l]
        h = _ffn_call(h, mod, wf[:, :D_FF].astype(BF16), wf[:, D_FF:].astype(BF16),
                      ffn_w_out[l].astype(BF16), ffn_conv_w[l], ffn_conv_b[l], ln2_g[l], ln2_b[l],
                      nb, skip_ctx=(l == depth - 1))
    return h
```

```python
import functools
import math

import jax
import jax.numpy as jnp
from jax import lax
from jax.experimental import pallas as pl
from jax.experimental.pallas import tpu as pltpu

F32 = jnp.float32
BF16 = jnp.bfloat16

D_MODEL = 1024
DEPTH = 4
GRID_W = 64
CTX_LEN = 256
GROUP_W = 512
A_HEAD = 128
A_HEADS = 4
LB_FLOOR = 1e-30
B_HEADDIM = 64
B_HEADS = 8
B_GROUPS = 2
B_STATE = 128
XBC_W = GROUP_W + 2 * B_GROUPS * B_STATE
C_BLOCKS = 8
C_BLOCK = 64
C_POW = 8.0
D_HEAD = 128
D_HEADS = 4
ROPE_BASE = 10000.0
D_FF = 2816
ALPHA = (2 * DEPTH) ** 0.25
EPS = 1e-6
LOG2E = 1.4426950408889634

TM = 256
FFN_HALO = 8
A_CHUNK = 64
A_SUB = 8
A_FBLK = 16
A_GUARD = 80.0
B_CHUNK = 128
D_CHUNK = 128
FF_CHUNK = 256
FFN_GBUF = 4
VMEM_LIMIT = 56 * 1024 * 1024

COL_AQ, COL_AI, COL_AF, COL_AG = 0, 512, 1024, 2048
COL_BZ, COL_BXBC = 2560, 3072
COL_CX, COL_CG = 4096, 4608
COL_DQ, COL_DK, COL_DV, COL_DG = 5120, 5632, 6144, 6656
W_DT = 7168
W_COLS = 7296
DT_W = 128
COL_AHI = 7168
COL_DT = 9216
P_W = 9344
CONV_LO, CONV_HI = COL_BXBC, COL_CG


def _cparams(sem):
    return pltpu.CompilerParams(dimension_semantics=sem, vmem_limit_bytes=VMEM_LIMIT)


def _softplus(x):
    return jnp.maximum(x, 0.0) + jnp.log(1.0 + jnp.exp(-jnp.abs(x)))


def _sigmoid(x):
    return 1.0 / (1.0 + jnp.exp(-x))


def _silu(x):
    return x * _sigmoid(x)


def _gelu_tanh(x):
    return 0.5 * x * (1.0 + jnp.tanh(math.sqrt(2.0 / math.pi) * (x + 0.044715 * (x * x * x))))


def _dot(a, b):
    return jnp.dot(a, b, preferred_element_type=F32)


def _dot_nt(a, b):
    return lax.dot_general(a, b, (((1,), (1,)), ((), ())), preferred_element_type=F32)


def _split3(x):
    hi = x.astype(BF16)
    r1 = x - hi.astype(F32)
    mid = r1.astype(BF16)
    lo = (r1 - mid.astype(F32)).astype(BF16)
    return hi, mid, lo


def _sel_dot(m01, x):
    hi, mid, lo = _split3(x)
    return _dot(m01, hi) + _dot(m01, mid) + _dot(m01, lo)


_DONE = object()


def _seq_block(d, j, nblk):
    return jnp.where(d == 0, j, jnp.where(j == 0, 0, nblk - j))


def _ada_kernel(c_ref, w_ref, b_ref, o_ref):
    s = _silu(c_ref[...])
    o_ref[0] = jnp.dot(s, w_ref[0], preferred_element_type=F32,
                       precision=lax.Precision.HIGHEST) + b_ref[0]


def _ada_call(cond, ada_w, ada_b):
    depth, d, n6 = ada_w.shape
    rows = cond.shape[0]
    tn = 1536
    return pl.pallas_call(
        _ada_kernel,
        grid=(depth, n6 // tn),
        in_specs=[pl.BlockSpec((rows, d), lambda l, n: (0, 0)),
                  pl.BlockSpec((1, d, tn), lambda l, n: (l, 0, n)),
                  pl.BlockSpec((1, 1, tn), lambda l, n: (l, 0, n))],
        out_specs=pl.BlockSpec((1, rows, tn), lambda l, n: (l, 0, n)),
        out_shape=jax.ShapeDtypeStruct((depth, rows, n6), F32),
        compiler_params=_cparams(("parallel", "parallel")),
        name="ada_mod",
    )(cond, ada_w, ada_b.reshape(depth, 1, n6))


def _inproj_kernel(h_ref, hp_ref, hn_ref, mod_ref, w_ref, cw_ref, cb_ref, lb_ref, cos_ref, sin_ref, o_ref, g_ref,
                   *, nb, tn, nblk):
    rb = pl.program_id(1)
    prev_ok = rb >= 2
    next_ok = (rb >= 1) & (rb <= nblk - 2)
    m = mod_ref[:, 0]
    sc, sh = 1.0 + m[:, 1:2, :], m[:, 0:1, :]
    hrows = TM + 2 * FFN_HALO
    ub = (h_ref[...] * sc + sh).astype(BF16).reshape(nb * TM, D_MODEL)
    u_all = (jnp.concatenate([hp_ref[...], h_ref[...], hn_ref[...]], axis=1) * sc + sh).astype(BF16)
    u_all = u_all.reshape(nb * hrows, D_MODEL)
    r1 = lax.broadcasted_iota(jnp.int32, (1, hrows, 1), 1)
    keep = ((r1 >= FFN_HALO) | prev_ok) & ((r1 < FFN_HALO + TM) | next_ok)
    heavy = [COL_AF, COL_AF + GROUP_W, COL_DQ, COL_DK] + list(range(CONV_LO, CONV_HI, tn))
    plain = [n for n in range(0, W_COLS, tn) if n not in heavy]
    order = [n for pair in zip(heavy, plain) for n in pair] + plain[len(heavy):] + heavy[len(plain):]
    for n0 in order:
        w = min(tn, W_COLS - n0)
        if CONV_LO <= n0 < CONV_HI:
            cs = slice(n0 - CONV_LO, n0 - CONV_LO + tn)
            g = _dot(u_all, w_ref[:, n0:n0 + tn]).reshape(nb, hrows, tn)
            g_ref[...] = jnp.where(keep, g, 0.0)
            acc = cb_ref[:, cs].reshape(1, 1, tn)
            for jj in range(4):
                acc = acc + cw_ref[jj:jj + 1, cs].reshape(1, 1, tn) * g_ref[:, pl.ds(FFN_HALO - 2 + jj, TM), :]
            if n0 < COL_CX:
                acc = _silu(acc)
            o_ref[:, :, n0:n0 + tn] = acc.astype(BF16)
            continue
        res = _dot(ub, w_ref[:, n0:n0 + w])
        if n0 in (COL_AF, COL_AF + GROUP_W):
            dd = (n0 - COL_AF) // GROUP_W
            lbv = lb_ref[dd:dd + 1, :]
            e = jnp.exp(-jnp.abs(res))
            big = 1.0 / (1.0 + e)
            small = e * big
            pos = res >= 0.0
            lf2 = jnp.log2(jnp.maximum(lbv, LB_FLOOR) + (1.0 - lbv) * jnp.where(pos, big, small))
            hi = lf2.astype(BF16)
            lo = (lf2 - hi.astype(F32)).astype(BF16)
            c_hi = COL_AHI + 2 * GROUP_W * dd
            o_ref[:, :, c_hi:c_hi + GROUP_W] = hi.reshape(nb, TM, GROUP_W)
            o_ref[:, :, c_hi + GROUP_W:c_hi + 2 * GROUP_W] = lo.reshape(nb, TM, GROUP_W)
            res = (1.0 - lbv) * jnp.where(pos, small, big)
        elif n0 in (COL_DQ, COL_DK):
            res = res.reshape(nb, TM, w)
            cosf, sinf = cos_ref[...], sin_ref[...]
            heads = []
            for hh in range(D_HEADS):
                samples = []
                for s in range(nb):
                    x = res[s, :, hh * D_HEAD:(hh + 1) * D_HEAD]
                    samples.append(x * cosf + pltpu.roll(x, D_HEAD // 2, 1) * sinf)
                heads.append(jnp.stack(samples))
            res = jnp.concatenate(heads, axis=2)
            if n0 == COL_DK:
                res = res * (D_HEAD ** -0.5)
        c_out = COL_DT if n0 == W_DT else n0
        o_ref[:, :, c_out:c_out + w] = res.astype(BF16).reshape(nb, TM, w)


def _inproj_call(h, mod, w_in_b, conv_w, conv_b, lb, cosf, sinf, nb):
    b, l, d = h.shape
    nblk = l // TM
    per = TM // FFN_HALO
    last = l // FFN_HALO - 1
    ncv = CONV_HI - CONV_LO
    kern = functools.partial(_inproj_kernel, nb=nb, tn=512, nblk=nblk)
    return pl.pallas_call(
        kern,
        grid=(b // nb, nblk),
        in_specs=[pl.BlockSpec((nb, TM, d), lambda i, t: (i, t, 0)),
                  pl.BlockSpec((nb, FFN_HALO, d), lambda i, t: (i, jnp.maximum(t * per - 1, 0), 0)),
                  pl.BlockSpec((nb, FFN_HALO, d), lambda i, t: (i, jnp.minimum((t + 1) * per, last), 0)),
                  pl.BlockSpec((nb, 1, 6, d), lambda i, t: (i, jnp.minimum(t, 1), 0, 0)),
                  pl.BlockSpec((d, W_COLS), lambda i, t: (0, 0), pipeline_mode=pl.Buffered(1)),
                  pl.BlockSpec((4, ncv), lambda i, t: (0, 0)),
                  pl.BlockSpec((1, ncv), lambda i, t: (0, 0)),
                  pl.BlockSpec((2, GROUP_W), lambda i, t: (0, 0)),
                  pl.BlockSpec((TM, D_HEAD), lambda i, t: (t, 0)),
                  pl.BlockSpec((TM, D_HEAD), lambda i, t: (t, 0))],
        out_specs=pl.BlockSpec((nb, TM, P_W), lambda i, t: (i, t, 0)),
        out_shape=jax.ShapeDtypeStruct((b, l, P_W), BF16),
        scratch_shapes=[pltpu.VMEM((nb, TM + 2 * FFN_HALO, 512), F32)],
        compiler_params=_cparams(("parallel", "parallel")),
        name="in_proj",
    )(h, h, h, mod, w_in_b, conv_w, conv_b, lb, cosf, sinf)


def _mixa_body(q_ref, v_ref, k_ref, hi_ref, lo_ref, y_ref, st_ref, sh_ref, rev):
    C, m = A_CHUNK, A_SUB
    nsl = C // m - 1

    row = lax.broadcasted_iota(jnp.int32, (C, C), 0)
    col = lax.broadcasted_iota(jnp.int32, (C, C), 1)
    tau_r = (C - 1 - row) if rev else row
    tau_c = (C - 1 - col) if rev else col
    blk_r, blk_c = tau_r // m, tau_c // m
    tri = (tau_c <= tau_r).astype(BF16)
    refrow = (C - m * blk_r) if rev else (m * blk_r - 1)
    rsel = ((col == refrow) & (blk_r >= 1)).astype(BF16)
    off_mask = blk_c < blk_r
    r1 = lax.broadcasted_iota(jnp.int32, (C, 1), 0)
    tau1 = (C - 1 - r1) if rev else r1
    blk1 = tau1 // m
    off1 = tau1 % m

    pad = m

    def chunk(i, carry):
        c = (TM // C - 1 - i) if rev else i
        sl = pl.ds(pl.multiple_of(c * C, C), C)
        q = q_ref[0, sl, :].astype(F32)
        v = v_ref[0, sl, :].astype(F32)
        k = k_ref[0, sl, :].astype(F32)
        log_f = (hi_ref[0, sl, :].astype(F32) + lo_ref[0, sl, :].astype(F32)) * (1.0 / LOG2E)
        b = _sel_dot(tri, log_f)
        rr = _sel_dot(rsel, b)
        qt = q * jnp.exp(jnp.minimum(b - rr, 0.0))
        sh_ref[0, pl.ds(pad, C), :] = k
        sh_ref[1, pl.ds(pad, C), :] = b
        sh_ref[2, pl.ds(pad, C), :] = v
        y_parts = []
        for h in range(A_HEADS):
            hs = slice(h * A_HEAD, (h + 1) * A_HEAD)
            qh, kh, vh, bh, qth = q[:, hs], k[:, hs], v[:, hs], b[:, hs], qt[:, hs]
            tot = bh[0:1, :] if rev else bh[C - 1:C, :]
            q_aug = jnp.concatenate(
                [jnp.where(blk1 == i, qth, 0.0) for i in range(1, nsl + 1)], axis=1).astype(BF16)
            k_parts = []
            for i in range(1, nsl + 1):
                rrow = (C - m * i) if rev else (m * i - 1)
                k_parts.append(kh * jnp.exp(jnp.minimum(bh[rrow:rrow + 1, :] - bh, 0.0)))
            k_aug = jnp.concatenate(k_parts, axis=1).astype(BF16)
            p_off = jnp.where(off_mask, _dot_nt(q_aug, k_aug), 0.0)
            y = _dot(p_off.astype(BF16), vh.astype(BF16))
            for dl in range(m):
                if dl == 0:
                    ks, bs, vs = kh, bh, vh
                else:
                    st = pad + dl if rev else pad - dl
                    ks = sh_ref[0, pl.ds(st, C), hs]
                    bs = sh_ref[1, pl.ds(st, C), hs]
                    vs = sh_ref[2, pl.ds(st, C), hs]
                tmp = qh * ks * jnp.exp(jnp.minimum(bh - bs, 0.0))
                p = jnp.sum(tmp, axis=-1, keepdims=True)
                y = y + jnp.where(off1 >= dl, p, 0.0) * vs
            st_t = st_ref[h]
            y = y + _dot_nt((qh * jnp.exp(bh)).astype(BF16), st_t.astype(BF16))
            kdec = (kh * jnp.exp(tot - bh)).astype(BF16)
            st_ref[h] = st_t * jnp.exp(tot) + _dot(vh.T.astype(BF16), kdec)
            y_parts.append(y)
        y_ref[0, 0, sl, :] = jnp.concatenate(y_parts, axis=1).astype(BF16)
        return carry

    lax.fori_loop(0, TM // C, chunk, 0)


def _mixa_tables(tm_ref, code_ref, rev):
    C, m = A_CHUNK, A_FBLK
    row = lax.broadcasted_iota(jnp.int32, (TM, TM), 0)
    col = lax.broadcasted_iota(jnp.int32, (TM, TM), 1)
    same_chunk = (row // C) == (col // C)
    tr, tc = row % C, col % C
    if rev:
        tr, tc = C - 1 - tr, C - 1 - tc
    same_blk = same_chunk & ((tr // m) == (tc // m))
    orr, oc = tr % m, tc % m
    half = m // 2
    tm_ref[0] = (same_chunk & (tc <= tr)).astype(BF16)
    tm_ref[1] = (same_blk & (oc <= orr)).astype(BF16)
    plus = same_blk & (oc >= half) & (oc <= orr)
    minus = same_blk & (oc > orr) & (oc < half)
    tm_ref[2] = (plus.astype(F32) - minus.astype(F32)).astype(BF16)
    off = same_chunk & ((tc // m) < (tr // m))
    diag = same_blk & (tc <= tr)
    code_ref[...] = jnp.where(off, 1.0, jnp.where(diag, 2.0, 0.0))


def _mixa_fast(q_ref, v_ref, y_ref, st_ref, code_ref, k_ref, b_ref, bl_ref, bm_ref, rev):
    C, m = A_CHUNK, A_FBLK
    nch = TM // C
    nsl = C // m - 1
    code = code_ref[...]
    off_mask = code == 1.0
    diag_mask = code == 2.0
    r1 = lax.broadcasted_iota(jnp.int32, (TM, 1), 0) % C
    blk1 = ((C - 1 - r1) if rev else r1) // m
    rows = lambda a, c: a[c * C:(c + 1) * C, :]
    order = range(nch - 1, -1, -1) if rev else range(nch)
    y_parts = []
    for h in range(A_HEADS):
        hs = slice(h * A_HEAD, (h + 1) * A_HEAD)
        qh = q_ref[0, :, hs].astype(F32)
        vb = v_ref[0, :, hs]
        kh = k_ref[0, :, hs].astype(F32)
        bh, blh, bmh = b_ref[:, hs], bl_ref[:, hs], bm_ref[:, hs]
        qt = qh * jnp.exp2(blh)
        q_aug = jnp.concatenate(
            [jnp.where(blk1 == i, qt, 0.0) for i in range(1, nsl + 1)], axis=1).astype(BF16)
        k_parts = []
        for i in range(1, nsl + 1):
            rrow = (C - m * i) if rev else (m * i - 1)
            dif = jnp.concatenate([bh[c * C + rrow:c * C + rrow + 1, :] - rows(bh, c) for c in range(nch)], axis=0)
            k_parts.append(kh * jnp.exp2(jnp.minimum(dif, 0.0)))
        k_aug = jnp.concatenate(k_parts, axis=1).astype(BF16)
        yield
        p = jnp.where(off_mask, _dot_nt(q_aug, k_aug), 0.0)
        qd = (qh * jnp.exp2(bmh)).astype(BF16)
        kd = (kh * jnp.exp2(-bmh)).astype(BF16)
        p = p + jnp.where(diag_mask, _dot_nt(qd, kd), 0.0)
        yield
        y = _dot(p.astype(BF16), vb)
        yield
        qe = (qh * jnp.exp2(bh)).astype(BF16)
        tots = [bh[c * C:c * C + 1, :] if rev else bh[(c + 1) * C - 1:(c + 1) * C, :] for c in range(nch)]
        upd = [_dot(rows(vb, c).astype(F32).T.astype(BF16),
                    (rows(kh, c) * jnp.exp2(tots[c] - rows(bh, c))).astype(BF16)) for c in range(nch)]
        yield
        st = st_ref[h]
        y_inter = [None] * nch
        for c in order:
            y_inter[c] = _dot_nt(rows(qe, c), st.astype(BF16))
            st = st * jnp.exp2(tots[c]) + upd[c]
        st_ref[h] = st
        y_parts.append(y + jnp.concatenate(y_inter, axis=0))
        yield
    y_ref[0, 0] = jnp.concatenate(y_parts, axis=1).astype(BF16)


def _mixa_prepare(hi_ref, lo_ref, tm_ref, b_ref, bl_ref, bm_ref, result):
    worst = jnp.zeros((1, 1), F32)
    lw = 2 * A_HEAD
    for h in range(GROUP_W // lw):
        hs = slice(h * lw, (h + 1) * lw)
        hi = hi_ref[0, :, hs]
        lo = lo_ref[0, :, hs]
        cum = lambda i: _dot(tm_ref[i], hi) + _dot(tm_ref[i], lo)
        b_ref[:, hs] = cum(0)
        bl_ref[:, hs] = cum(1)
        bm = cum(2)
        bm_ref[:, hs] = bm
        worst = jnp.maximum(worst, jnp.max(jnp.abs(bm), keepdims=True))
        yield
    result.append(worst[0, 0] <= A_GUARD * LOG2E)


def _mixb_body(xc_ref, dt_ref, dtb_ref, alog_ref, e_ref, dsk_ref, y_ref, st_ref, tri_ref, rev, d):
    C = B_CHUNK
    hpg = B_HEADS // B_GROUPS
    gw = hpg * B_HEADDIM
    row = lax.broadcasted_iota(jnp.int32, (C, C), 0)
    col = lax.broadcasted_iota(jnp.int32, (C, C), 1)
    trimask = (col >= row) if rev else (col <= row)
    tri = tri_ref[...]
    lane_head = lax.broadcasted_iota(jnp.int32, (1, gw), 1) // B_HEADDIM
    e01 = e_ref[0]
    dtb = dtb_ref[0]
    neg_a = -jnp.exp(alog_ref[0])

    def expand(x):
        hi, mid, lo = _split3(x)
        return _dot(hi, e01) + _dot(mid, e01) + _dot(lo, e01)

    for c in (range(TM // C - 1, -1, -1) if rev else range(TM // C)):
        sl = pl.ds(c * C, C)
        bx = xc_ref[0, sl, 0:GROUP_W].astype(F32)
        bb = xc_ref[0, sl, GROUP_W:GROUP_W + B_GROUPS * B_STATE]
        bc = xc_ref[0, sl, GROUP_W + B_GROUPS * B_STATE:XBC_W]
        dt = _softplus(dt_ref[0, sl, :].astype(F32) + dtb)
        la = dt * neg_a
        b = _sel_dot(tri, la)
        b_exp = expand(b)
        dt_exp = expand(dt)
        tot_exp = b_exp[0:1, :] if rev else b_exp[C - 1:C, :]
        v_all = bx * dt_exp
        vdec = v_all * jnp.exp(tot_exp - b_exp)
        eb = jnp.exp(b_exp)
        bt = b.T
        yield
        y_parts = []
        for g in range(B_GROUPS):
            gs = slice(g * gw, (g + 1) * gw)
            bg = bb[:, g * B_STATE:(g + 1) * B_STATE]
            cg = bc[:, g * B_STATE:(g + 1) * B_STATE]
            qk = _dot_nt(cg, bg)
            yield
            s_g = st_ref[g]
            vg = v_all[:, gs].astype(BF16)
            m_parts, v_parts = [], []
            for e in range(hpg):
                lane = 8 * d + g * hpg + e
                rel = jnp.where(trimask,
                                jnp.exp(jnp.minimum(b[:, lane:lane + 1] - bt[lane:lane + 1, :], 0.0)), 0.0)
                m_parts.append((qk * rel).astype(BF16))
                v_parts.append(jnp.where(lane_head == e, vg, jnp.zeros_like(vg)))
            y_g = (_dot(jnp.concatenate(m_parts, axis=1), jnp.concatenate(v_parts, axis=0))
                   + _dot(cg, s_g.astype(BF16)) * eb[:, gs])
            yield
            st_ref[g] = (s_g * jnp.exp(tot_exp[:, gs])
                         + _dot(bg.astype(F32).T.astype(BF16), vdec[:, gs].astype(BF16)))
            y_parts.append(y_g)
            yield
        y = jnp.concatenate(y_parts, axis=1)
        if not rev:
            y = y + dsk_ref[...] * bx
        y_ref[0, 0, sl, :] = y.astype(BF16)


def _mixc_body(y_ref, carry_ref, a_ref, b_ref, rev):
    grp = TM // 8
    a3 = a_ref[...].reshape(grp, 8, GROUP_W)
    b3 = b_ref[...].reshape(grp, 8, GROUP_W)
    ro = lax.broadcasted_iota(jnp.int32, (1, 8, 1), 1)
    for s in (1, 2, 4):
        valid = (ro + s <= 7) if rev else (ro >= s)
        shift = (8 - s) if rev else s
        a_sh = pltpu.roll(a3, shift, 1)
        b_sh = pltpu.roll(b3, shift, 1)
        b3 = b3 + a3 * jnp.where(valid, b_sh, 0.0)
        a3 = a3 * jnp.where(valid, a_sh, 1.0)
        yield
    cb = jnp.broadcast_to(carry_ref[...], (8, GROUP_W))
    last = 0 if rev else 7
    hs = [None] * grp
    for g in (range(grp - 1, -1, -1) if rev else range(grp)):
        hg = a3[g] * cb + b3[g]
        hs[g] = hg
        cb = jnp.broadcast_to(hg[last:last + 1, :], (8, GROUP_W))
        if g % 8 == 0:
            yield
    carry_ref[...] = cb[0:1, :]
    y_ref[0, 0] = jnp.concatenate(hs, axis=0).astype(BF16)


def _mixc_gates(x_ref, wa_ref, wi_ref, ba_ref, bi_ref, lam_ref, a_ref, b_ref):
    cxb = x_ref[0]
    cx = cxb.astype(F32)
    r = _sigmoid(_dot(cxb, wa_ref[0]) + ba_ref[0])
    gi = _sigmoid(_dot(cxb, wi_ref[0]) + bi_ref[0])
    yield
    la = (-C_POW) * r * _softplus(-lam_ref[0])
    a = jnp.exp(la)
    a_ref[...] = a
    b_ref[...] = jnp.sqrt(1.0 - a * a) * gi * cx
    yield


def _block_diag(w):
    nd, nb_, n, _ = w.shape
    eye = jnp.eye(nb_, dtype=w.dtype)
    return jnp.einsum("dhij,hg->dhigj", w, eye).reshape(nd, nb_ * n, nb_ * n)


def _mixd_body(q_ref, k_ref, v_ref, lgt_ref, y_ref, st_ref, rev):
    C = D_CHUNK
    lg = -_softplus(-lgt_ref[0])
    row = lax.broadcasted_iota(jnp.int32, (C, C), 0)
    col = lax.broadcasted_iota(jnp.int32, (C, C), 1)
    trimask = (col >= row) if rev else (col <= row)
    dist = jnp.maximum((col - row) if rev else (row - col), 0).astype(F32)
    r1 = lax.broadcasted_iota(jnp.int32, (C, 1), 0)
    tau1 = ((C - 1 - r1) if rev else r1).astype(F32)
    for c in (range(TM // C - 1, -1, -1) if rev else range(TM // C)):
        sl = pl.ds(c * C, C)
        y_parts = []
        for h in range(D_HEADS):
            hs = slice(h * D_HEAD, (h + 1) * D_HEAD)
            lgh = lg[h:h + 1, 0:1]
            qb = q_ref[0, sl, hs]
            kb = k_ref[0, sl, hs]
            vh = v_ref[0, sl, hs]
            kr = kb.astype(F32)
            qk = _dot_nt(qb, kb)
            yield
            rel = jnp.where(trimask, jnp.exp(lgh * dist), 0.0)
            s_h = st_ref[h]
            y = _dot((qk * rel).astype(BF16), vh) + _dot(qb, s_h.astype(BF16)) * jnp.exp(lgh * (tau1 + 1.0))
            kdec = kr * jnp.exp(lgh * (C - 1.0 - tau1))
            st_ref[h] = s_h * jnp.exp(lgh * float(C)) + _dot(kdec.T.astype(BF16), vh)
            y_parts.append(y)
            yield
        y_ref[0, 0, sl, :] = jnp.concatenate(y_parts, axis=1).astype(BF16)


def _mix_kernel(aq_ref, ai_ref, ak_ref, ahi_ref, alo_ref,
                bx_ref, bdt_ref, dtb_ref, alog_ref, e_ref, dsk_ref,
                cx_ref, wa_ref, wi_ref, ba_ref, bi_ref, lam_ref,
                dq_ref, dk_ref, dv_ref, lgt_ref,
                ya_ref, yb_ref, yc_ref, yd_ref,
                a_st, a_sh, a_tm, a_code, a_b, a_bl, a_bm, b_st, c_carry, c_a, c_b, d_st, a_bak, b_tri,
                *, nb):
    d = pl.program_id(1)
    j = pl.program_id(2)

    @pl.when(j == 0)
    def _():
        for r in (a_st, a_sh, b_st, c_carry, d_st):
            r[...] = jnp.zeros_like(r)

    def run(rev, dd):
        @pl.when(j == 0)
        def _():
            _mixa_tables(a_tm, a_code, rev)
            row = lax.broadcasted_iota(jnp.int32, (B_CHUNK, B_CHUNK), 0)
            col = lax.broadcasted_iota(jnp.int32, (B_CHUNK, B_CHUNK), 1)
            b_tri[...] = ((col >= row) if rev else (col <= row)).astype(BF16)

        a_bak[...] = a_st[...]
        oks = []
        streams = []
        for s in range(nb):
            one = lambda r, s=s: r.at[pl.ds(s, 1)]
            out = lambda r, s=s: r.at[:, pl.ds(s, 1)]

            def mixer_a(s=s, one=one, out=out):
                yield from _mixa_prepare(one(ahi_ref), one(alo_ref), a_tm, a_b.at[s], a_bl.at[s], a_bm.at[s], oks)
                yield from _mixa_fast(one(aq_ref), one(ai_ref), out(ya_ref), a_st.at[s], a_code, one(ak_ref),
                                      a_b.at[s], a_bl.at[s], a_bm.at[s], rev)

            def mixer_c(s=s, one=one, out=out):
                yield from _mixc_gates(one(cx_ref), wa_ref, wi_ref, ba_ref, bi_ref, lam_ref, c_a.at[s], c_b.at[s])
                yield from _mixc_body(out(yc_ref), c_carry.at[s], c_a.at[s], c_b.at[s], rev)

            streams += [mixer_a(),
                        _mixb_body(one(bx_ref), one(bdt_ref), dtb_ref, alog_ref, e_ref, dsk_ref, out(yb_ref),
                                   b_st.at[s], b_tri, rev, dd),
                        mixer_c(),
                        _mixd_body(one(dq_ref), one(dk_ref), one(dv_ref), lgt_ref, out(yd_ref), d_st.at[s], rev)]
        while streams:
            for g in list(streams):
                if next(g, _DONE) is _DONE:
                    streams.remove(g)

        for s in range(nb):
            @pl.when(jnp.logical_not(oks[s]))
            def _():
                a_st[s] = a_bak[s]
                _mixa_body(aq_ref.at[pl.ds(s, 1)], ai_ref.at[pl.ds(s, 1)], ak_ref.at[pl.ds(s, 1)],
                           ahi_ref.at[pl.ds(s, 1)], alo_ref.at[pl.ds(s, 1)],
                           ya_ref.at[:, pl.ds(s, 1)], a_st.at[s], a_sh, rev)

    @pl.when(d == 0)
    def _():
        run(False, 0)

    @pl.when(d == 1)
    def _():
        run(True, 1)


def _mix_call(p, dt_bias, a_log, ssm_d, wa, ba, wi, bi, lam, decay_logit, nb):
    b, l, _ = p.shape
    nblk = l // TM
    rowblk = lambda d, j: _seq_block(d, j, nblk)
    pcol = lambda col, width=GROUP_W: pl.BlockSpec((nb, TM, width), lambda i, d, j: (i, rowblk(d, j), col // width))
    dircol = lambda col, step: pl.BlockSpec((nb, TM, GROUP_W),
                                            lambda i, d, j: (i, rowblk(d, j), col // GROUP_W + step * d))
    dirvec = lambda width: pl.BlockSpec((1, 1, width), lambda i, d, j: (d, 0, 0))
    pad_dir = lambda a: jnp.stack([jnp.pad(a[0], (0, DT_W - B_HEADS)),
                                   jnp.pad(a[1], (B_HEADS, DT_W - 2 * B_HEADS))]).reshape(2, 1, DT_W)
    lanes = jnp.arange(GROUP_W) // B_HEADDIM
    e01 = jnp.stack([(jnp.arange(DT_W)[:, None] == lanes[None, :] + 8 * dd) for dd in range(2)]).astype(BF16)
    dsk = jnp.repeat(ssm_d, B_HEADDIM).reshape(1, GROUP_W)
    vec = lambda a: a.reshape(2, 1, GROUP_W)
    wspec = pl.BlockSpec((1, GROUP_W, GROUP_W), lambda i, d, j: (d, 0, 0))
    lgt = jnp.broadcast_to(jnp.pad(decay_logit, ((0, 0), (0, 8 - D_HEADS)))[:, :, None], (2, 8, D_HEAD))
    yspec = pl.BlockSpec((1, nb, TM, GROUP_W), lambda i, d, j: (d, i, rowblk(d, j), 0))
    yshape = jax.ShapeDtypeStruct((2, b, l, GROUP_W), BF16)
    act = lambda: pltpu.VMEM((nb, TM, GROUP_W), F32)
    return pl.pallas_call(
        functools.partial(_mix_kernel, nb=nb),
        grid=(b // nb, 2, nblk),
        in_specs=[pcol(COL_AQ), pcol(COL_AI), dircol(COL_AF, 1), dircol(COL_AHI, 2), dircol(COL_AHI + GROUP_W, 2),
                  pcol(COL_BXBC, XBC_W), pcol(COL_DT, DT_W), dirvec(DT_W), dirvec(DT_W),
                  pl.BlockSpec((1, DT_W, GROUP_W), lambda i, d, j: (d, 0, 0)),
                  pl.BlockSpec((1, GROUP_W), lambda i, d, j: (0, 0)),
                  pcol(COL_CX), wspec, wspec, dirvec(GROUP_W), dirvec(GROUP_W), dirvec(GROUP_W),
                  pcol(COL_DQ), pcol(COL_DK), pcol(COL_DV),
                  pl.BlockSpec((1, 8, D_HEAD), lambda i, d, j: (d, 0, 0))],
        out_specs=[yspec, yspec, yspec, yspec],
        out_shape=[yshape, yshape, yshape, yshape],
        scratch_shapes=[pltpu.VMEM((nb, A_HEADS, A_HEAD, A_HEAD), F32),
                        pltpu.VMEM((3, A_CHUNK + 2 * A_SUB, GROUP_W), F32),
                        pltpu.VMEM((3, TM, TM), BF16),
                        pltpu.VMEM((TM, TM), F32),
                        act(), act(), act(),
                        pltpu.VMEM((nb, B_GROUPS, B_STATE, GROUP_W // B_GROUPS), F32),
                        pltpu.VMEM((nb, 1, GROUP_W), F32), act(), act(),
                        pltpu.VMEM((nb, D_HEADS, D_HEAD, D_HEAD), F32),
                        pltpu.VMEM((nb, A_HEADS, A_HEAD, A_HEAD), F32),
                        pltpu.VMEM((B_CHUNK, B_CHUNK), BF16)],
        compiler_params=_cparams(("parallel", "parallel", "arbitrary")),
        name="mixers",
    )(p, p, p, p, p,
      p, p, pad_dir(dt_bias), pad_dir(a_log), e01, dsk,
      p, _block_diag(wa).astype(BF16), _block_diag(wi).astype(BF16), vec(ba), vec(bi), vec(lam),
      p, p, p, lgt)


def _layer_norm_rows(x, g, b):
    mu = jnp.mean(x, axis=-1, keepdims=True)
    xc = x - mu
    var = jnp.mean(xc * xc, axis=-1, keepdims=True)
    return xc * lax.rsqrt(var + EPS) * g + b


def _outproj_kernel(yaf, yab, ybf, ybb, ycf, ycb, ydf, ydb, ag_ref, bz_ref, cg_ref, dg_ref,
                    h_ref, mod_ref, w_ref, nwa_ref, nwb_ref, lg_ref, lb_ref, o_ref, *, nb):
    rows = nb * TM
    ld = lambda f, bk: (f[0].astype(F32) + bk[0].astype(F32)).reshape(rows, GROUP_W)
    gate = lambda r: r[...].astype(F32).reshape(rows, GROUP_W)
    ya = ld(yaf, yab)
    parts = []
    for h in range(A_HEADS):
        x = ya[:, h * A_HEAD:(h + 1) * A_HEAD]
        parts.append(x * lax.rsqrt(jnp.mean(x * x, axis=-1, keepdims=True) + EPS))
    oa = jnp.concatenate(parts, axis=1) * nwa_ref[...] * _silu(gate(ag_ref))
    yb = ld(ybf, ybb) * _silu(gate(bz_ref))
    gw = GROUP_W // B_GROUPS
    parts = []
    for g in range(B_GROUPS):
        x = yb[:, g * gw:(g + 1) * gw]
        parts.append(x * lax.rsqrt(jnp.mean(x * x, axis=-1, keepdims=True) + EPS))
    ob = jnp.concatenate(parts, axis=1) * nwb_ref[...]
    oc = ld(ycf, ycb) * _gelu_tanh(gate(cg_ref))
    yd = ld(ydf, ydb)
    parts = []
    for h in range(D_HEADS):
        x = yd[:, h * D_HEAD:(h + 1) * D_HEAD]
        mu = jnp.mean(x, axis=-1, keepdims=True)
        xc = x - mu
        parts.append(xc * lax.rsqrt(jnp.mean(xc * xc, axis=-1, keepdims=True) + EPS))
    od = jnp.concatenate(parts, axis=1) * _silu(gate(dg_ref))
    mix = (_dot(oa.astype(BF16), w_ref[0:GROUP_W, :])
           + _dot(ob.astype(BF16), w_ref[GROUP_W:2 * GROUP_W, :])
           + _dot(oc.astype(BF16), w_ref[2 * GROUP_W:3 * GROUP_W, :])
           + _dot(od.astype(BF16), w_ref[3 * GROUP_W:4 * GROUP_W, :]))
    m = mod_ref[:, 0]
    hn = ALPHA * h_ref[...] + m[:, 2:3, :] * mix.reshape(nb, TM, D_MODEL)
    o_ref[...] = _layer_norm_rows(hn, lg_ref[...], lb_ref[...])


def _outproj_call(ya, yb, yc, yd, p, h, mod, w_out_b, nwa, nwb, ln_g, ln_b, nb):
    b, l, d = h.shape
    ydir = lambda dd: pl.BlockSpec((1, nb, TM, GROUP_W), lambda i, t: (dd, i, t, 0))
    pcol = lambda col: pl.BlockSpec((nb, TM, GROUP_W), lambda i, t: (i, t, col // GROUP_W))
    vec = lambda n: pl.BlockSpec((1, n), lambda i, t: (0, 0))
    return pl.pallas_call(
        functools.partial(_outproj_kernel, nb=nb),
        grid=(b // nb, l // TM),
        in_specs=[ydir(0), ydir(1), ydir(0), ydir(1), ydir(0), ydir(1), ydir(0), ydir(1),
                  pcol(COL_AG), pcol(COL_BZ), pcol(COL_CG), pcol(COL_DG),
                  pl.BlockSpec((nb, TM, d), lambda i, t: (i, t, 0)),
                  pl.BlockSpec((nb, 1, 6, d), lambda i, t: (i, jnp.minimum(t, 1), 0, 0)),
                  pl.BlockSpec((4 * GROUP_W, d), lambda i, t: (0, 0), pipeline_mode=pl.Buffered(1)),
                  vec(GROUP_W), vec(GROUP_W), vec(d), vec(d)],
        out_specs=pl.BlockSpec((nb, TM, d), lambda i, t: (i, t, 0)),
        out_shape=jax.ShapeDtypeStruct((b, l, d), F32),
        compiler_params=_cparams(("parallel", "parallel")),
        name="out_proj",
    )(ya, ya, yb, yb, yc, yc, yd, yd, p, p, p, p, h, mod, w_out_b,
      nwa.reshape(1, GROUP_W), nwb.reshape(1, GROUP_W), ln_g.reshape(1, d), ln_b.reshape(1, d))


def _ffn_kernel(h_ref, hp_ref, hn_ref, mod_ref, wg_ref, wu_ref, wo_ref, cw_ref, cb_ref, lg_ref, lb_ref,
                o_ref, act_ref, g_ref, *, nb, nblk, first_blk):
    rb = pl.program_id(1) + first_blk
    prev_ok = rb >= 2
    next_ok = (rb >= 1) & (rb <= nblk - 2)
    m = mod_ref[:, 0]
    sh, sc = m[:, 3:4, :], m[:, 4:5, :]
    hm = h_ref[...]
    hrows = TM + 2 * FFN_HALO
    x_all = jnp.concatenate([hp_ref[...], hm, hn_ref[...]], axis=1) * (1.0 + sc) + sh
    xb_all = x_all.astype(BF16).reshape(nb * hrows, D_MODEL)
    xb = x_all[:, FFN_HALO:FFN_HALO + TM, :].astype(BF16).reshape(nb * TM, D_MODEL)
    r1 = lax.broadcasted_iota(jnp.int32, (1, hrows, 1), 1)
    keep = ((r1 >= FFN_HALO) | prev_ok) & ((r1 < FFN_HALO + TM) | next_ok)
    for c0 in range(0, D_FF, FF_CHUNK):
        cs = slice(c0, c0 + FF_CHUNK)
        g = _dot(xb_all, wg_ref[:, cs]).reshape(nb, hrows, FF_CHUNK)
        gi = (c0 // FF_CHUNK) % FFN_GBUF
        g_ref[gi] = jnp.where(keep, g, 0.0)
        gc = cb_ref[:, cs].reshape(1, 1, FF_CHUNK)
        for jj in range(3):
            gc = gc + (cw_ref[jj:jj + 1, cs].reshape(1, 1, FF_CHUNK)
                       * g_ref[gi, :, pl.ds(FFN_HALO - 1 + jj, TM), :])
        up = _dot(xb, wu_ref[:, cs])
        act_ref[:, cs] = (_gelu_tanh(gc).reshape(nb * TM, FF_CHUNK) * up).astype(BF16)
    f = _dot(act_ref[...], wo_ref[...])
    hn2 = ALPHA * hm + m[:, 5:6, :] * f.reshape(nb, TM, D_MODEL)
    o_ref[...] = _layer_norm_rows(hn2, lg_ref[...], lb_ref[...])


def _ffn_call(h, mod, wg_b, wu_b, wo_b, conv_w, conv_b, ln_g, ln_b, nb, skip_ctx):
    b, l, d = h.shape
    nblk = l // TM
    first = 1 if skip_ctx else 0
    per = TM // FFN_HALO
    last = l // FFN_HALO - 1
    vec = lambda n: pl.BlockSpec((1, n), lambda i, t: (0, 0))
    res = lambda shape: pl.BlockSpec(shape, lambda i, t: (0, 0), pipeline_mode=pl.Buffered(1))
    return pl.pallas_call(
        functools.partial(_ffn_kernel, nb=nb, nblk=nblk, first_blk=first),
        grid=(b // nb, nblk - first),
        in_specs=[pl.BlockSpec((nb, TM, d), lambda i, t: (i, t + first, 0)),
                  pl.BlockSpec((nb, FFN_HALO, d), lambda i, t: (i, jnp.maximum((t + first) * per - 1, 0), 0)),
                  pl.BlockSpec((nb, FFN_HALO, d), lambda i, t: (i, jnp.minimum((t + first + 1) * per, last), 0)),
                  pl.BlockSpec((nb, 1, 6, d), lambda i, t: (i, jnp.minimum(t + first, 1), 0, 0)),
                  res((d, D_FF)), res((d, D_FF)), res((D_FF, d)),
                  pl.BlockSpec((3, D_FF), lambda i, t: (0, 0)), vec(D_FF), vec(d), vec(d)],
        out_specs=pl.BlockSpec((nb, TM, d), lambda i, t: (i, t, 0)),
        out_shape=jax.ShapeDtypeStruct((b, l - first * TM, d), F32),
        scratch_shapes=[pltpu.VMEM((nb * TM, D_FF), BF16),
                        pltpu.VMEM((FFN_GBUF, nb, TM + 2 * FFN_HALO, FF_CHUNK), F32)],
        compiler_params=_cparams(("parallel", "parallel")),
        name="conv_ffn",
    )(h, h, h, mod, wg_b, wu_b, wo_b, conv_w, conv_b.reshape(1, D_FF), ln_g.reshape(1, d), ln_b.reshape(1, d))


def _rope_tables(n):
    rows = n // GRID_W
    row = jnp.repeat(jnp.arange(rows), GRID_W)
    col = jnp.tile(jnp.arange(GRID_W), rows)
    n_freq = D_HEAD // 4
    inv = ROPE_BASE ** (-jnp.arange(n_freq, dtype=F32) / n_freq)
    ang = jnp.concatenate([row[:, None] * inv, col[:, None] * inv], axis=-1)
    ang = jnp.concatenate([jnp.zeros((CTX_LEN, D_HEAD // 2), F32), ang], axis=0)
    cos, sin = jnp.cos(ang), jnp.sin(ang)
    return jnp.concatenate([cos, cos], axis=-1), jnp.concatenate([-sin, sin], axis=-1)


def kernel(x, c, ctx, c_ctx, ada_w, ada_b, w_in, hgrn_lb_logits, hgrn_norm_w, ssm_conv_w, ssm_conv_b, ssm_dt_bias, ssm_a_log, ssm_d, ssm_norm_w, lru_conv_w, lru_conv_b, lru_wa, lru_ba, lru_wi, lru_bi, lru_lambda, ret_decay_logit, w_out, ln1_g, ln1_b, ffn_w_in, ffn_conv_w, ffn_conv_b, ffn_w_out, ln2_g, ln2_b):
    bsz, n, d = x.shape
    assert d == D_MODEL and ctx.shape[1] == CTX_LEN == TM and n % TM == 0 and n % GRID_W == 0
    nb = 2 if bsz % 2 == 0 else 1
    depth = ada_w.shape[0]

    cosf, sinf = _rope_tables(n)
    p_soft = jax.nn.softmax(hgrn_lb_logits.astype(F32), axis=0)
    lb_all = jnp.cumsum(p_soft, axis=0) - p_soft

    rows = -(-(bsz + 1) // 8) * 8
    cond = jnp.zeros((rows, d), F32).at[:bsz].set(c).at[bsz].set(c_ctx)
    mods = _ada_call(cond, ada_w, ada_b)

    n_in = w_in.shape[2]
    dt0 = COL_BXBC + XBC_W
    h = jnp.concatenate([ctx, x], axis=1)
    for l in range(depth):
        ml = mods[l].reshape(rows, 6, d)
        mod = jnp.stack([jnp.broadcast_to(ml[bsz][None], (bsz, 6, d)), ml[:bsz]], axis=1)
        w_l = w_in[l]
        w_perm = jnp.concatenate([w_l[:, :dt0], w_l[:, dt0 + 2 * B_HEADS:], w_l[:, dt0:dt0 + 2 * B_HEADS],
                                  jnp.zeros((d, W_COLS - n_in), F32)], axis=1).astype(BF16)
        conv_w = jnp.concatenate([ssm_conv_w[l], lru_conv_w[l]], axis=1)
        conv_b = jnp.concatenate([ssm_conv_b[l], lru_conv_b[l]]).reshape(1, CONV_HI - CONV_LO)
        p = _inproj_call(h, mod, w_perm, conv_w, conv_b, lb_all[l], cosf, sinf, nb)
        ya, yb, yc, yd = _mix_call(p, ssm_dt_bias[l], ssm_a_log[l], ssm_d[l],
                                   lru_wa[l], lru_ba[l], lru_wi[l], lru_bi[l], lru_lambda[l],
                                   ret_decay_logit[l].astype(F32), nb)
        h = _outproj_call(ya, yb, yc, yd, p, h, mod, w_out[l].astype(BF16), hgrn_norm_w[l], ssm_norm_w[l],
                          ln1_g[l], ln1_b[l], nb)
        wf = ffn_w_in[l]
        h = _ffn_call(h, mod, wf[:, :D_FF].astype(BF16), wf[:, D_FF:].astype(BF16),
                      ffn_w_out[l].astype(BF16), ffn_conv_w[l], ffn_conv_b[l], ln2_g[l], ln2_b[l],
                      nb, skip_ctx=(l == depth - 1))
    return h
```

```python
import functools
import math

import jax
import jax.numpy as jnp
from jax import lax
from jax.experimental import pallas as pl
from jax.experimental.pallas import tpu as pltpu

F32 = jnp.float32
BF16 = jnp.bfloat16

D_MODEL = 1024
DEPTH = 4
GRID_W = 64
CTX_LEN = 256
GROUP_W = 512
A_HEAD = 128
A_HEADS = 4
LB_FLOOR = 1e-30
B_HEADDIM = 64
B_HEADS = 8
B_GROUPS = 2
B_STATE = 128
XBC_W = GROUP_W + 2 * B_GROUPS * B_STATE
C_BLOCKS = 8
C_BLOCK = 64
C_POW = 8.0
D_HEAD = 128
D_HEADS = 4
ROPE_BASE = 10000.0
D_FF = 2816
ALPHA = (2 * DEPTH) ** 0.25
EPS = 1e-6
LOG2E = 1.4426950408889634
LN2 = 0.6931471805599453

TM = 256
FFN_HALO = 8
A_CHUNK = 64
A_SUB = 8
A_FBLK = 16
A_GUARD = 80.0
B_CHUNK = 128
D_CHUNK = 128
FF_CHUNK = 256
FFN_GBUF = 4
VMEM_LIMIT = 56 * 1024 * 1024

COL_AQ, COL_AI, COL_AF, COL_AG = 0, 512, 1024, 2048
COL_BZ, COL_BXBC = 2560, 3072
COL_CX, COL_CG = 4096, 4608
COL_DQ, COL_DK, COL_DV, COL_DG = 5120, 5632, 6144, 6656
W_DT = 7168
W_COLS = 7296
DT_W = 128
COL_AHI = 7168
COL_DT = 9216
P_W = 9344
CONV_LO, CONV_HI = COL_BXBC, COL_CG


def _cparams(sem):
    return pltpu.CompilerParams(dimension_semantics=sem, vmem_limit_bytes=VMEM_LIMIT)


def _softplus(x):
    return jnp.maximum(x, 0.0) + jnp.log2(1.0 + jnp.exp(-jnp.abs(x))) * LN2


def _sigmoid(x):
    return 1.0 / (1.0 + jnp.exp(-x))


def _silu(x):
    return x * _sigmoid(x)


def _gelu_tanh(x):
    return 0.5 * x * (1.0 + jnp.tanh(math.sqrt(2.0 / math.pi) * (x + 0.044715 * (x * x * x))))


def _dot(a, b):
    return jnp.dot(a, b, preferred_element_type=F32)


def _dot_nt(a, b):
    return lax.dot_general(a, b, (((1,), (1,)), ((), ())), preferred_element_type=F32)


def _split3(x):
    hi = x.astype(BF16)
    r1 = x - hi.astype(F32)
    mid = r1.astype(BF16)
    lo = (r1 - mid.astype(F32)).astype(BF16)
    return hi, mid, lo


def _sel_dot(m01, x):
    hi, mid, lo = _split3(x)
    return _dot(m01, hi) + _dot(m01, mid) + _dot(m01, lo)


_DONE = object()


def _seq_block(d, j, nblk):
    return jnp.where(d == 0, j, jnp.where(j == 0, 0, nblk - j))


def _ada_kernel(c_ref, w_ref, b_ref, o_ref):
    s = _silu(c_ref[...])
    o_ref[0] = jnp.dot(s, w_ref[0], preferred_element_type=F32,
                       precision=lax.Precision.HIGHEST) + b_ref[0]


def _ada_call(cond, ada_w, ada_b):
    depth, d, n6 = ada_w.shape
    rows = cond.shape[0]
    tn = 1536
    return pl.pallas_call(
        _ada_kernel,
        grid=(depth, n6 // tn),
        in_specs=[pl.BlockSpec((rows, d), lambda l, n: (0, 0)),
                  pl.BlockSpec((1, d, tn), lambda l, n: (l, 0, n)),
                  pl.BlockSpec((1, 1, tn), lambda l, n: (l, 0, n))],
        out_specs=pl.BlockSpec((1, rows, tn), lambda l, n: (l, 0, n)),
        out_shape=jax.ShapeDtypeStruct((depth, rows, n6), F32),
        compiler_params=_cparams(("parallel", "parallel")),
        name="ada_mod",
    )(cond, ada_w, ada_b.reshape(depth, 1, n6))


def _inproj_kernel(h_ref, hp_ref, hn_ref, mod_ref, w_ref, cw_ref, cb_ref, lb_ref, cos_ref, sin_ref, o_ref, g_ref,
                   *, nb, tn, nblk):
    rb = pl.program_id(1)
    prev_ok = rb >= 2
    next_ok = (rb >= 1) & (rb <= nblk - 2)
    m = mod_ref[:, 0]
    sc, sh = 1.0 + m[:, 1:2, :], m[:, 0:1, :]
    hrows = TM + 2 * FFN_HALO
    ub = (h_ref[...] * sc + sh).astype(BF16).reshape(nb * TM, D_MODEL)
    u_all = (jnp.concatenate([hp_ref[...], h_ref[...], hn_ref[...]], axis=1) * sc + sh).astype(BF16)
    u_all = u_all.reshape(nb * hrows, D_MODEL)
    r1 = lax.broadcasted_iota(jnp.int32, (1, hrows, 1), 1)
    keep = ((r1 >= FFN_HALO) | prev_ok) & ((r1 < FFN_HALO + TM) | next_ok)
    heavy = [COL_AF, COL_AF + GROUP_W, COL_DQ, COL_DK] + list(range(CONV_LO, CONV_HI, tn))
    plain = [n for n in range(0, W_COLS, tn) if n not in heavy]
    order = [n for pair in zip(heavy, plain) for n in pair] + plain[len(heavy):] + heavy[len(plain):]
    for n0 in order:
        w = min(tn, W_COLS - n0)
        res = _dot(u_all if CONV_LO <= n0 < CONV_HI else ub, w_ref[:, n0:n0 + w])
        if CONV_LO <= n0 < CONV_HI:
            cs = slice(n0 - CONV_LO, n0 - CONV_LO + tn)
            g = res.reshape(nb, hrows, tn)
            g_ref[...] = jnp.where(keep, g, 0.0)
            acc = cb_ref[:, cs].reshape(1, 1, tn)
            for jj in range(4):
                acc = acc + cw_ref[jj:jj + 1, cs].reshape(1, 1, tn) * g_ref[:, pl.ds(FFN_HALO - 2 + jj, TM), :]
            if n0 < COL_CX:
                acc = _silu(acc)
            o_ref[:, :, n0:n0 + tn] = acc.astype(BF16)
            continue
        if n0 in (COL_AF, COL_AF + GROUP_W):
            dd = (n0 - COL_AF) // GROUP_W
            lbv = lb_ref[dd:dd + 1, :]
            e = jnp.exp(-jnp.abs(res))
            big = 1.0 / (1.0 + e)
            small = e * big
            pos = res >= 0.0
            lf2 = jnp.log2(jnp.maximum(lbv, LB_FLOOR) + (1.0 - lbv) * jnp.where(pos, big, small))
            hi = lf2.astype(BF16)
            lo = (lf2 - hi.astype(F32)).astype(BF16)
            c_hi = COL_AHI + 2 * GROUP_W * dd
            o_ref[:, :, c_hi:c_hi + GROUP_W] = hi.reshape(nb, TM, GROUP_W)
            o_ref[:, :, c_hi + GROUP_W:c_hi + 2 * GROUP_W] = lo.reshape(nb, TM, GROUP_W)
            res = (1.0 - lbv) * jnp.where(pos, small, big)
        elif n0 in (COL_DQ, COL_DK):
            res = res.reshape(nb, TM, w)
            cosf, sinf = cos_ref[...], sin_ref[...]
            heads = []
            for hh in range(D_HEADS):
                samples = []
                for s in range(nb):
                    x = res[s, :, hh * D_HEAD:(hh + 1) * D_HEAD]
                    samples.append(x * cosf + pltpu.roll(x, D_HEAD // 2, 1) * sinf)
                heads.append(jnp.stack(samples))
            res = jnp.concatenate(heads, axis=2)
            if n0 == COL_DK:
                res = res * (D_HEAD ** -0.5)
        c_out = COL_DT if n0 == W_DT else n0
        o_ref[:, :, c_out:c_out + w] = res.astype(BF16).reshape(nb, TM, w)


def _inproj_call(h, mod, w_in_b, conv_w, conv_b, lb, cosf, sinf, nb):
    b, l, d = h.shape
    nblk = l // TM
    per = TM // FFN_HALO
    last = l // FFN_HALO - 1
    ncv = CONV_HI - CONV_LO
    kern = functools.partial(_inproj_kernel, nb=nb, tn=512, nblk=nblk)
    return pl.pallas_call(
        kern,
        grid=(b // nb, nblk),
        in_specs=[pl.BlockSpec((nb, TM, d), lambda i, t: (i, t, 0)),
                  pl.BlockSpec((nb, FFN_HALO, d), lambda i, t: (i, jnp.maximum(t * per - 1, 0), 0)),
                  pl.BlockSpec((nb, FFN_HALO, d), lambda i, t: (i, jnp.minimum((t + 1) * per, last), 0)),
                  pl.BlockSpec((nb, 1, 6, d), lambda i, t: (i, jnp.minimum(t, 1), 0, 0)),
                  pl.BlockSpec((d, W_COLS), lambda i, t: (0, 0), pipeline_mode=pl.Buffered(1)),
                  pl.BlockSpec((4, ncv), lambda i, t: (0, 0)),
                  pl.BlockSpec((1, ncv), lambda i, t: (0, 0)),
                  pl.BlockSpec((2, GROUP_W), lambda i, t: (0, 0)),
                  pl.BlockSpec((TM, D_HEAD), lambda i, t: (t, 0)),
                  pl.BlockSpec((TM, D_HEAD), lambda i, t: (t, 0))],
        out_specs=pl.BlockSpec((nb, TM, P_W), lambda i, t: (i, t, 0)),
        out_shape=jax.ShapeDtypeStruct((b, l, P_W), BF16),
        scratch_shapes=[pltpu.VMEM((nb, TM + 2 * FFN_HALO, 512), F32)],
        compiler_params=_cparams(("parallel", "parallel")),
        name="in_proj",
    )(h, h, h, mod, w_in_b, conv_w, conv_b, lb, cosf, sinf)


def _mixa_body(q_ref, v_ref, k_ref, hi_ref, lo_ref, y_ref, st_ref, sh_ref, rev):
    C, m = A_CHUNK, A_SUB
    nsl = C // m - 1

    row = lax.broadcasted_iota(jnp.int32, (C, C), 0)
    col = lax.broadcasted_iota(jnp.int32, (C, C), 1)
    tau_r = (C - 1 - row) if rev else row
    tau_c = (C - 1 - col) if rev else col
    blk_r, blk_c = tau_r // m, tau_c // m
    tri = (tau_c <= tau_r).astype(BF16)
    refrow = (C - m * blk_r) if rev else (m * blk_r - 1)
    rsel = ((col == refrow) & (blk_r >= 1)).astype(BF16)
    off_mask = blk_c < blk_r
    r1 = lax.broadcasted_iota(jnp.int32, (C, 1), 0)
    tau1 = (C - 1 - r1) if rev else r1
    blk1 = tau1 // m
    off1 = tau1 % m

    pad = m

    def chunk(i, carry):
        c = (TM // C - 1 - i) if rev else i
        sl = pl.ds(pl.multiple_of(c * C, C), C)
        q = q_ref[0, sl, :].astype(F32)
        v = v_ref[0, sl, :].astype(F32)
        k = k_ref[0, sl, :].astype(F32)
        log_f = (hi_ref[0, sl, :].astype(F32) + lo_ref[0, sl, :].astype(F32)) * (1.0 / LOG2E)
        b = _sel_dot(tri, log_f)
        rr = _sel_dot(rsel, b)
        qt = q * jnp.exp(jnp.minimum(b - rr, 0.0))
        sh_ref[0, pl.ds(pad, C), :] = k
        sh_ref[1, pl.ds(pad, C), :] = b
        sh_ref[2, pl.ds(pad, C), :] = v
        y_parts = []
        for h in range(A_HEADS):
            hs = slice(h * A_HEAD, (h + 1) * A_HEAD)
            qh, kh, vh, bh, qth = q[:, hs], k[:, hs], v[:, hs], b[:, hs], qt[:, hs]
            tot = bh[0:1, :] if rev else bh[C - 1:C, :]
            q_aug = jnp.concatenate(
                [jnp.where(blk1 == i, qth, 0.0) for i in range(1, nsl + 1)], axis=1).astype(BF16)
            k_parts = []
            for i in range(1, nsl + 1):
                rrow = (C - m * i) if rev else (m * i - 1)
                k_parts.append(kh * jnp.exp(jnp.minimum(bh[rrow:rrow + 1, :] - bh, 0.0)))
            k_aug = jnp.concatenate(k_parts, axis=1).astype(BF16)
            p_off = jnp.where(off_mask, _dot_nt(q_aug, k_aug), 0.0)
            y = _dot(p_off.astype(BF16), vh.astype(BF16))
            for dl in range(m):
                if dl == 0:
                    ks, bs, vs = kh, bh, vh
                else:
                    st = pad + dl if rev else pad - dl
                    ks = sh_ref[0, pl.ds(st, C), hs]
                    bs = sh_ref[1, pl.ds(st, C), hs]
                    vs = sh_ref[2, pl.ds(st, C), hs]
                tmp = qh * ks * jnp.exp(jnp.minimum(bh - bs, 0.0))
                p = jnp.sum(tmp, axis=-1, keepdims=True)
                y = y + jnp.where(off1 >= dl, p, 0.0) * vs
            st_t = st_ref[h]
            y = y + _dot_nt((qh * jnp.exp(bh)).astype(BF16), st_t.astype(BF16))
            kdec = (kh * jnp.exp(tot - bh)).astype(BF16)
            st_ref[h] = st_t * jnp.exp(tot) + _dot(vh.T.astype(BF16), kdec)
            y_parts.append(y)
        y_ref[0, 0, sl, :] = jnp.concatenate(y_parts, axis=1).astype(BF16)
        return carry

    lax.fori_loop(0, TM // C, chunk, 0)


def _mixa_tables(tm_ref, code_ref, rev):
    C, m = A_CHUNK, A_FBLK
    row = lax.broadcasted_iota(jnp.int32, (TM, TM), 0)
    col = lax.broadcasted_iota(jnp.int32, (TM, TM), 1)
    same_chunk = (row // C) == (col // C)
    tr, tc = row % C, col % C
    if rev:
        tr, tc = C - 1 - tr, C - 1 - tc
    same_blk = same_chunk & ((tr // m) == (tc // m))
    tm_ref[0] = (same_chunk & (tc <= tr)).astype(BF16)
    off =same_chunk & ((tc // m) < (tr // m))
    diag = same_blk & (tc <= tr)
    code_ref[...] = jnp.where(off, 1.0, jnp.where(diag, 2.0, 0.0))


def _mixa_fast(q_ref, v_ref, y_ref, st_ref, code_ref, k_ref, b_ref, bl_ref, bm_ref, rev):
    C, m = A_CHUNK, A_FBLK
    nch = TM // C
    nsl = C // m - 1
    code = code_ref[...]
    off_mask = code == 1.0
    diag_mask = code == 2.0
    r1 = lax.broadcasted_iota(jnp.int32, (TM, 1), 0) % C
    blk1 = ((C - 1 - r1) if rev else r1) // m
    rows = lambda a, c: a[c * C:(c + 1) * C, :]
    order = range(nch - 1, -1, -1) if rev else range(nch)
    y_parts = []
    for h in range(A_HEADS):
        hs = slice(h * A_HEAD, (h + 1) * A_HEAD)
        qh = q_ref[0, :, hs].astype(F32)
        vb = v_ref[0, :, hs]
        kh = k_ref[0, :, hs].astype(F32)
        bh, blh, bmh = b_ref[:, hs], bl_ref[:, hs], bm_ref[:, hs]
        qt = qh * jnp.exp2(blh)
        q_aug = jnp.concatenate(
            [jnp.where(blk1 == i, qt, 0.0) for i in range(1, nsl + 1)], axis=1).astype(BF16)
        k_parts = []
        for i in range(1, nsl + 1):
            rrow = (C - m * i) if rev else (m * i - 1)
            dif = jnp.concatenate([bh[c * C + rrow:c * C + rrow + 1, :] - rows(bh, c) for c in range(nch)], axis=0)
            k_parts.append(kh * jnp.exp2(jnp.minimum(dif, 0.0)))
        k_aug = jnp.concatenate(k_parts, axis=1).astype(BF16)
        yield
        p = jnp.where(off_mask, _dot_nt(q_aug, k_aug), 0.0)
        qd = (qh * jnp.exp2(bmh)).astype(BF16)
        kd = (kh * jnp.exp2(-bmh)).astype(BF16)
        p = p + jnp.where(diag_mask, _dot_nt(qd, kd), 0.0)
        yield
        y = _dot(p.astype(BF16), vb)
        yield
        qe = (qh * jnp.exp2(bh)).astype(BF16)
        tots = [bh[c * C:c * C + 1, :] if rev else bh[(c + 1) * C - 1:(c + 1) * C, :] for c in range(nch)]
        upd = [_dot(rows(vb, c).astype(F32).T.astype(BF16),
                    (rows(kh, c) * jnp.exp2(tots[c] - rows(bh, c))).astype(BF16)) for c in range(nch)]
        yield
        st = st_ref[h]
        y_inter = [None] * nch
        for c in order:
            y_inter[c] = _dot_nt(rows(qe, c), st.astype(BF16))
            st = st * jnp.exp2(tots[c]) + upd[c]
        st_ref[h] = st
        y_parts.append(y + jnp.concatenate(y_inter, axis=0))
        yield
    y_ref[0, 0] = jnp.concatenate(y_parts, axis=1).astype(BF16)


def _mixa_prepare(hi_ref, lo_ref, tm_ref, b_ref, bl_ref, bm_ref, result, rev):
    C, m = A_CHUNK, A_FBLK
    worst = jnp.zeros((1, 1), F32)
    lw = 2 * A_HEAD
    for h in range(GROUP_W // lw):
        hs = slice(h * lw, (h + 1) * lw)
        b = _dot(tm_ref[0], hi_ref[0, :, hs]) + _dot(tm_ref[0], lo_ref[0, :, hs])
        starts, mids = [], []
        for r0 in range(0, TM, m):
            pb = (r0 % C) // m
            first = (pb == C // m - 1) if rev else (pb == 0)
            prev_row = (r0 + m) if rev else (r0 - 1)
            mid_row = (r0 + m // 2) if rev else (r0 + m // 2 - 1)
            start = jnp.zeros((1, lw), F32) if first else b[prev_row:prev_row + 1, :]
            starts.append(jnp.broadcast_to(start, (m, lw)))
            mids.append(jnp.broadcast_to(b[mid_row:mid_row + 1, :], (m, lw)))
        bm = b - jnp.concatenate(mids, axis=0)
        b_ref[:, hs] = b
        bl_ref[:, hs] = b - jnp.concatenate(starts, axis=0)
        bm_ref[:, hs] = bm
        worst = jnp.maximum(worst, jnp.max(jnp.abs(bm), keepdims=True))
        yield
    result.append(worst[0, 0] <= A_GUARD * LOG2E)


def _mixb_body(xc_ref, dt_ref, dtb_ref, alog_ref, e_ref, dsk_ref, y_ref, st_ref, tri_ref, rev, d):
    C = B_CHUNK
    hpg = B_HEADS // B_GROUPS
    gw = hpg * B_HEADDIM
    row = lax.broadcasted_iota(jnp.int32, (C, C), 0)
    col = lax.broadcasted_iota(jnp.int32, (C, C), 1)
    trimask = (col >= row) if rev else (col <= row)
    tri = tri_ref[...]
    lane_head = lax.broadcasted_iota(jnp.int32, (1, gw), 1) // B_HEADDIM
    e01 = e_ref[0]
    dtb = dtb_ref[0]
    neg_a = -jnp.exp(alog_ref[0])

    def expand(x):
        hi = x.astype(BF16)
        lo = (x - hi.astype(F32)).astype(BF16)
        return _dot(hi, e01) + _dot(lo, e01)

    for c in (range(TM // C - 1, -1, -1) if rev else range(TM // C)):
        sl = pl.ds(c * C, C)
        bx = xc_ref[0, sl, 0:GROUP_W].astype(F32)
        bb = xc_ref[0, sl, GROUP_W:GROUP_W + B_GROUPS * B_STATE]
        bc = xc_ref[0, sl, GROUP_W + B_GROUPS * B_STATE:XBC_W]
        dt = _softplus(dt_ref[0, sl, :].astype(F32) + dtb)
        la = dt * neg_a
        b = _sel_dot(tri, la)
        b_exp = expand(b)
        dt_exp = _dot(dt.astype(BF16), e01)
        tot_exp = b_exp[0:1, :] if rev else b_exp[C - 1:C, :]
        v_all = bx * dt_exp
        vdec = v_all * jnp.exp(tot_exp - b_exp)
        eb = jnp.exp(b_exp)
        bt = b.T
        yield
        y_parts = []
        for g in range(B_GROUPS):
            gs = slice(g * gw, (g + 1) * gw)
            bg = bb[:, g * B_STATE:(g + 1) * B_STATE]
            cg = bc[:, g * B_STATE:(g + 1) * B_STATE]
            qk = _dot_nt(cg, bg)
            yield
            s_g = st_ref[g]
            vg = v_all[:, gs].astype(BF16)
            m_parts, v_parts = [], []
            for e in range(hpg):
                lane = 8 * d + g * hpg + e
                rel = jnp.where(trimask,
                                jnp.exp(jnp.minimum(b[:, lane:lane + 1] - bt[lane:lane + 1, :], 0.0)), 0.0)
                m_parts.append((qk * rel).astype(BF16))
                v_parts.append(jnp.where(lane_head == e, vg, jnp.zeros_like(vg)))
            y_g = (_dot(jnp.concatenate(m_parts, axis=1), jnp.concatenate(v_parts, axis=0))
                   + _dot(cg, s_g.astype(BF16)) * eb[:, gs])
            yield
            st_ref[g] = (s_g * jnp.exp(tot_exp[:, gs])
                         + _dot(bg.astype(F32).T.astype(BF16), vdec[:, gs].astype(BF16)))
            y_parts.append(y_g)
            yield
        y = jnp.concatenate(y_parts, axis=1)
        if not rev:
            y = y + dsk_ref[...] * bx
        y_ref[0, 0, sl, :] = y.astype(BF16)


def _mixc_body(y_ref, carry_ref, a_ref, b_ref, rev):
    grp = TM // 8
    a3 = a_ref[...].reshape(grp, 8, GROUP_W)
    b3 = b_ref[...].reshape(grp, 8, GROUP_W)
    ro = lax.broadcasted_iota(jnp.int32, (1, 8, 1), 1)
    for s in (1, 2, 4):
        valid = (ro + s <= 7) if rev else (ro >= s)
        shift = (8 - s) if rev else s
        a_sh = pltpu.roll(a3, shift, 1)
        b_sh = pltpu.roll(b3, shift, 1)
        b3 = b3 + a3 * jnp.where(valid, b_sh, 0.0)
        a3 = a3 * jnp.where(valid, a_sh, 1.0)
        yield
    cb = jnp.broadcast_to(carry_ref[...], (8, GROUP_W))
    last = 0 if rev else 7
    hs = [None] * grp
    for g in (range(grp - 1, -1, -1) if rev else range(grp)):
        hg = a3[g] * cb + b3[g]
        hs[g] = hg
        cb = jnp.broadcast_to(hg[last:last + 1, :], (8, GROUP_W))
        if g % 8 == 0:
            yield
    carry_ref[...] = cb[0:1, :]
    y_ref[0, 0] = jnp.concatenate(hs, axis=0).astype(BF16)


def _mixc_gates(x_ref, wa_ref, wi_ref, ba_ref, bi_ref, lam_ref, a_ref, b_ref):
    cxb = x_ref[0]
    cx = cxb.astype(F32)
    r = _sigmoid(_dot(cxb, wa_ref[0]) + ba_ref[0])
    gi = _sigmoid(_dot(cxb, wi_ref[0]) + bi_ref[0])
    yield
    la = (-C_POW) * r * _softplus(-lam_ref[0])
    a = jnp.exp(la)
    a_ref[...] = a
    s = 1.0 - a * a
    root = jnp.where(s > 0.0, s * lax.rsqrt(s), 0.0)
    b_ref[...] = root * gi * cx
    yield


def _block_diag(w):
    nd, nb_, n, _ = w.shape
    eye = jnp.eye(nb_, dtype=w.dtype)
    return jnp.einsum("dhij,hg->dhigj", w, eye).reshape(nd, nb_ * n, nb_ * n)


def _mixd_body(q_ref, k_ref, v_ref, lgt_ref, y_ref, st_ref, rev):
    C = D_CHUNK
    lg = -_softplus(-lgt_ref[0])
    row = lax.broadcasted_iota(jnp.int32, (C, C), 0)
    col = lax.broadcasted_iota(jnp.int32, (C, C), 1)
    trimask = (col >= row) if rev else (col <= row)
    dist = jnp.maximum((col - row) if rev else (row - col), 0).astype(F32)
    r1 = lax.broadcasted_iota(jnp.int32, (C, 1), 0)
    tau1 = ((C - 1 - r1) if rev else r1).astype(F32)
    for c in (range(TM // C - 1, -1, -1) if rev else range(TM // C)):
        sl = pl.ds(c * C, C)
        y_parts = []
        for h in range(D_HEADS):
            hs = slice(h * D_HEAD, (h + 1) * D_HEAD)
            lgh = lg[h:h + 1, 0:1]
            qb = q_ref[0, sl, hs]
            kb = k_ref[0, sl, hs]
            vh = v_ref[0, sl, hs]
            kr = kb.astype(F32)
            qk = _dot_nt(qb, kb)
            yield
            rel = jnp.where(trimask, jnp.exp(lgh * dist), 0.0)
            s_h = st_ref[h]
            y = _dot((qk * rel).astype(BF16), vh) + _dot(qb, s_h.astype(BF16)) * jnp.exp(lgh * (tau1 + 1.0))
            kdec = kr * jnp.exp(lgh * (C - 1.0 - tau1))
            st_ref[h] = s_h * jnp.exp(lgh * float(C)) + _dot(kdec.T.astype(BF16), vh)
            y_parts.append(y)
            yield
        y_ref[0, 0, sl, :] = jnp.concatenate(y_parts, axis=1).astype(BF16)


def _mix_kernel(aq_ref, ai_ref, ak_ref, ahi_ref, alo_ref,
                bx_ref, bdt_ref, dtb_ref, alog_ref, e_ref, dsk_ref,
                cx_ref, wa_ref, wi_ref, ba_ref, bi_ref, lam_ref,
                dq_ref, dk_ref, dv_ref, lgt_ref,
                ya_ref, yb_ref, yc_ref, yd_ref,
                a_st, a_sh, a_tm, a_code, a_b, a_bl, a_bm, b_st, c_carry, c_a, c_b, d_st, a_bak, b_tri,
                *, nb):
    d = pl.program_id(1)
    j = pl.program_id(2)

    @pl.when(j == 0)
    def _():
        for r in (a_st, a_sh, b_st, c_carry, d_st):
            r[...] = jnp.zeros_like(r)

    def run(rev, dd):
        @pl.when(j == 0)
        def _():
            _mixa_tables(a_tm, a_code, rev)
            row = lax.broadcasted_iota(jnp.int32, (B_CHUNK, B_CHUNK), 0)
            col = lax.broadcasted_iota(jnp.int32, (B_CHUNK, B_CHUNK), 1)
            b_tri[...] = ((col >= row) if rev else (col <= row)).astype(BF16)

        a_bak[...] = a_st[...]
        oks = []
        streams = []
        for s in range(nb):
            one = lambda r, s=s: r.at[pl.ds(s, 1)]
            out = lambda r, s=s: r.at[:, pl.ds(s, 1)]

            def mixer_a(s=s, one=one, out=out):
                yield from _mixa_prepare(one(ahi_ref), one(alo_ref), a_tm, a_b.at[s], a_bl.at[s], a_bm.at[s], oks,
                                         rev)
                yield from _mixa_fast(one(aq_ref), one(ai_ref), out(ya_ref), a_st.at[s], a_code, one(ak_ref),
                                      a_b.at[s], a_bl.at[s], a_bm.at[s], rev)

            def mixer_c(s=s, one=one, out=out):
                yield from _mixc_gates(one(cx_ref), wa_ref, wi_ref, ba_ref, bi_ref, lam_ref, c_a.at[s], c_b.at[s])
                yield from _mixc_body(out(yc_ref), c_carry.at[s], c_a.at[s], c_b.at[s], rev)

            streams += [mixer_a(),
                        _mixb_body(one(bx_ref), one(bdt_ref), dtb_ref, alog_ref, e_ref, dsk_ref, out(yb_ref),
                                   b_st.at[s], b_tri, rev, dd),
                        mixer_c(),
                        _mixd_body(one(dq_ref), one(dk_ref), one(dv_ref), lgt_ref, out(yd_ref), d_st.at[s], rev)]
        while streams:
            for g in list(streams):
                if next(g, _DONE) is _DONE:
                    streams.remove(g)

        for s in range(nb):
            @pl.when(jnp.logical_not(oks[s]))
            def _():
                a_st[s] = a_bak[s]
                _mixa_body(aq_ref.at[pl.ds(s, 1)], ai_ref.at[pl.ds(s, 1)], ak_ref.at[pl.ds(s, 1)],
                           ahi_ref.at[pl.ds(s, 1)], alo_ref.at[pl.ds(s, 1)],
                           ya_ref.at[:, pl.ds(s, 1)], a_st.at[s], a_sh, rev)

    @pl.when(d == 0)
    def _():
        run(False, 0)

    @pl.when(d == 1)
    def _():
        run(True, 1)


def _mix_call(p, dt_bias, a_log, ssm_d, wa, ba, wi, bi, lam, decay_logit, nb):
    b, l, _ = p.shape
    nblk = l // TM
    rowblk = lambda d, j: _seq_block(d, j, nblk)
    pcol = lambda col, width=GROUP_W: pl.BlockSpec((nb, TM, width), lambda i, d, j: (i, rowblk(d, j), col // width))
    dircol = lambda col, step: pl.BlockSpec((nb, TM, GROUP_W),
                                            lambda i, d, j: (i, rowblk(d, j), col // GROUP_W + step * d))
    dirvec = lambda width: pl.BlockSpec((1, 1, width), lambda i, d, j: (d, 0, 0))
    pad_dir = lambda a: jnp.stack([jnp.pad(a[0], (0, DT_W - B_HEADS)),
                                   jnp.pad(a[1], (B_HEADS, DT_W - 2 * B_HEADS))]).reshape(2, 1, DT_W)
    lanes = jnp.arange(GROUP_W) // B_HEADDIM
    e01 = jnp.stack([(jnp.arange(DT_W)[:, None] == lanes[None, :] + 8 * dd) for dd in range(2)]).astype(BF16)
    dsk = jnp.repeat(ssm_d, B_HEADDIM).reshape(1, GROUP_W)
    vec = lambda a: a.reshape(2, 1, GROUP_W)
    wspec = pl.BlockSpec((1, GROUP_W, GROUP_W), lambda i, d, j: (d, 0, 0))
    lgt = jnp.broadcast_to(jnp.pad(decay_logit, ((0, 0), (0, 8 - D_HEADS)))[:, :, None], (2, 8, D_HEAD))
    yspec = pl.BlockSpec((1, nb, TM, GROUP_W), lambda i, d, j: (d, i, rowblk(d, j), 0))
    yshape = jax.ShapeDtypeStruct((2, b, l, GROUP_W), BF16)
    act = lambda: pltpu.VMEM((nb, TM, GROUP_W), F32)
    return pl.pallas_call(
        functools.partial(_mix_kernel, nb=nb),
        grid=(b // nb, 2, nblk),
        in_specs=[pcol(COL_AQ), pcol(COL_AI), dircol(COL_AF, 1), dircol(COL_AHI, 2), dircol(COL_AHI + GROUP_W, 2),
                  pcol(COL_BXBC, XBC_W), pcol(COL_DT, DT_W), dirvec(DT_W), dirvec(DT_W),
                  pl.BlockSpec((1, DT_W, GROUP_W), lambda i, d, j: (d, 0, 0)),
                  pl.BlockSpec((1, GROUP_W), lambda i, d, j: (0, 0)),
                  pcol(COL_CX), wspec, wspec, dirvec(GROUP_W), dirvec(GROUP_W), dirvec(GROUP_W),
                  pcol(COL_DQ), pcol(COL_DK), pcol(COL_DV),
                  pl.BlockSpec((1, 8, D_HEAD), lambda i, d, j: (d, 0, 0))],
        out_specs=[yspec, yspec, yspec, yspec],
        out_shape=[yshape, yshape, yshape, yshape],
        scratch_shapes=[pltpu.VMEM((nb, A_HEADS, A_HEAD, A_HEAD), F32),
                        pltpu.VMEM((3, A_CHUNK + 2 * A_SUB, GROUP_W), F32),
                        pltpu.VMEM((1, TM, TM), BF16),
                        pltpu.VMEM((TM, TM), F32),
                        act(), act(), act(),
                        pltpu.VMEM((nb, B_GROUPS, B_STATE, GROUP_W // B_GROUPS), F32),
                        pltpu.VMEM((nb, 1, GROUP_W), F32), act(), act(),
                        pltpu.VMEM((nb, D_HEADS, D_HEAD, D_HEAD), F32),
                        pltpu.VMEM((nb, A_HEADS, A_HEAD, A_HEAD), F32),
                        pltpu.VMEM((B_CHUNK, B_CHUNK), BF16)],
        compiler_params=_cparams(("parallel", "parallel", "arbitrary")),
        name="mixers",
    )(p, p, p, p, p,
      p, p, pad_dir(dt_bias), pad_dir(a_log), e01, dsk,
      p, _block_diag(wa).astype(BF16), _block_diag(wi).astype(BF16), vec(ba), vec(bi), vec(lam),
      p, p, p, lgt)


def _layer_norm_rows(x, g, b):
    mu = jnp.mean(x, axis=-1, keepdims=True)
    xc = x - mu
    var = jnp.mean(xc * xc, axis=-1, keepdims=True)
    return xc * lax.rsqrt(var + EPS) * g + b


def _outproj_kernel(yaf, yab, ybf, ybb, ycf, ycb, ydf, ydb, ag_ref, bz_ref, cg_ref, dg_ref,
                    h_ref, mod_ref, w_ref, nwa_ref, nwb_ref, lg_ref, lb_ref, o_ref, *, nb):
    rows = nb * TM
    ld = lambda f, bk: (f[0].astype(F32) + bk[0].astype(F32)).reshape(rows, GROUP_W)
    gate = lambda r: r[...].astype(F32).reshape(rows, GROUP_W)
    ya = ld(yaf, yab)
    parts = []
    for h in range(A_HEADS):
        x = ya[:, h * A_HEAD:(h + 1) * A_HEAD]
        parts.append(x * lax.rsqrt(jnp.mean(x * x, axis=-1, keepdims=True) + EPS))
    oa = jnp.concatenate(parts, axis=1) * nwa_ref[...] * _silu(gate(ag_ref))
    yb = ld(ybf, ybb) * _silu(gate(bz_ref))
    gw = GROUP_W // B_GROUPS
    parts = []
    for g in range(B_GROUPS):
        x = yb[:, g * gw:(g + 1) * gw]
        parts.append(x * lax.rsqrt(jnp.mean(x * x, axis=-1, keepdims=True) + EPS))
    ob = jnp.concatenate(parts, axis=1) * nwb_ref[...]
    oc = ld(ycf, ycb) * _gelu_tanh(gate(cg_ref))
    yd = ld(ydf, ydb)
    parts = []
    for h in range(D_HEADS):
        x = yd[:, h * D_HEAD:(h + 1) * D_HEAD]
        mu = jnp.mean(x, axis=-1, keepdims=True)
        xc = x - mu
        parts.append(xc * lax.rsqrt(jnp.mean(xc * xc, axis=-1, keepdims=True) + EPS))
    od = jnp.concatenate(parts, axis=1) * _silu(gate(dg_ref))
    mix = (_dot(oa.astype(BF16), w_ref[0:GROUP_W, :])
           + _dot(ob.astype(BF16), w_ref[GROUP_W:2 * GROUP_W, :])
           + _dot(oc.astype(BF16), w_ref[2 * GROUP_W:3 * GROUP_W, :])
           + _dot(od.astype(BF16), w_ref[3 * GROUP_W:4 * GROUP_W, :]))
    m = mod_ref[:, 0]
    hn = ALPHA * h_ref[...] + m[:, 2:3, :] * mix.reshape(nb, TM, D_MODEL)
    o_ref[...] = _layer_norm_rows(hn, lg_ref[...], lb_ref[...])


def _outproj_call(ya, yb, yc, yd, p, h, mod, w_out_b, nwa, nwb, ln_g, ln_b, nb):
    b, l, d = h.shape
    ydir = lambda dd: pl.BlockSpec((1, nb, TM, GROUP_W), lambda i, t: (dd, i, t, 0))
    pcol = lambda col: pl.BlockSpec((nb, TM, GROUP_W), lambda i, t: (i, t, col // GROUP_W))
    vec = lambda n: pl.BlockSpec((1, n), lambda i, t: (0, 0))
    return pl.pallas_call(
        functools.partial(_outproj_kernel, nb=nb),
        grid=(b // nb, l // TM),
        in_specs=[ydir(0), ydir(1), ydir(0), ydir(1), ydir(0), ydir(1), ydir(0), ydir(1),
                  pcol(COL_AG), pcol(COL_BZ), pcol(COL_CG), pcol(COL_DG),
                  pl.BlockSpec((nb, TM, d), lambda i, t: (i, t, 0)),
                  pl.BlockSpec((nb, 1, 6, d), lambda i, t: (i, jnp.minimum(t, 1), 0, 0)),
                  pl.BlockSpec((4 * GROUP_W, d), lambda i, t: (0, 0), pipeline_mode=pl.Buffered(1)),
                  vec(GROUP_W), vec(GROUP_W), vec(d), vec(d)],
        out_specs=pl.BlockSpec((nb, TM, d), lambda i, t: (i, t, 0)),
        out_shape=jax.ShapeDtypeStruct((b, l, d), F32),
        compiler_params=_cparams(("parallel", "parallel")),
        name="out_proj",
    )(ya, ya, yb, yb, yc, yc, yd, yd, p, p, p, p, h, mod, w_out_b,
      nwa.reshape(1, GROUP_W), nwb.reshape(1, GROUP_W), ln_g.reshape(1, d), ln_b.reshape(1, d))


def _ffn_kernel(h_ref, hp_ref, hn_ref, mod_ref, wg_ref, wu_ref, wo_ref, cw_ref, cb_ref, lg_ref, lb_ref,
                o_ref, act_ref, g_ref, *, nb, nblk, first_blk):
    rb = pl.program_id(1) + first_blk
    prev_ok = rb >= 2
    next_ok = (rb >= 1) & (rb <= nblk - 2)
    m = mod_ref[:, 0]
    sh, sc = m[:, 3:4, :], m[:, 4:5, :]
    hm = h_ref[...]
    hrows = TM + 2 * FFN_HALO
    x_all = jnp.concatenate([hp_ref[...], hm, hn_ref[...]], axis=1) * (1.0 + sc) + sh
    xb_all = x_all.astype(BF16).reshape(nb * hrows, D_MODEL)
    xb = x_all[:, FFN_HALO:FFN_HALO + TM, :].astype(BF16).reshape(nb * TM, D_MODEL)
    r1 = lax.broadcasted_iota(jnp.int32, (1, hrows, 1), 1)
    keep = ((r1 >= FFN_HALO) | prev_ok) & ((r1 < FFN_HALO + TM) | next_ok)
    for c0 in range(0, D_FF, FF_CHUNK):
        cs = slice(c0, c0 + FF_CHUNK)
        g = _dot(xb_all, wg_ref[:, cs]).reshape(nb, hrows, FF_CHUNK)
        up = _dot(xb, wu_ref[:, cs])
        gi = (c0 // FF_CHUNK) % FFN_GBUF
        g_ref[gi] = jnp.where(keep, g, 0.0)
        gc = cb_ref[:, cs].reshape(1, 1, FF_CHUNK)
        for jj in range(3):
            gc = gc + (cw_ref[jj:jj + 1, cs].reshape(1, 1, FF_CHUNK)
                       * g_ref[gi, :, pl.ds(FFN_HALO - 1 + jj, TM), :])
        act_ref[:, cs] =(_gelu_tanh(gc).reshape(nb * TM, FF_CHUNK) * up).astype(BF16)
    f = _dot(act_ref[...], wo_ref[...])
    hn2 = ALPHA * hm + m[:, 5:6, :] * f.reshape(nb, TM, D_MODEL)
    o_ref[...] = _layer_norm_rows(hn2, lg_ref[...], lb_ref[...])


def _ffn_call(h, mod, wg_b, wu_b, wo_b, conv_w, conv_b, ln_g, ln_b, nb, skip_ctx):
    b, l, d = h.shape
    nblk = l // TM
    first = 1 if skip_ctx else 0
    per = TM // FFN_HALO
    last = l // FFN_HALO - 1
    vec = lambda n: pl.BlockSpec((1, n), lambda i, t: (0, 0))
    res = lambda shape: pl.BlockSpec(shape, lambda i, t: (0, 0), pipeline_mode=pl.Buffered(1))
    return pl.pallas_call(
        functools.partial(_ffn_kernel, nb=nb, nblk=nblk, first_blk=first),
        grid=(b // nb, nblk - first),
        in_specs=[pl.BlockSpec((nb, TM, d), lambda i, t: (i, t + first, 0)),
                  pl.BlockSpec((nb, FFN_HALO, d), lambda i, t: (i, jnp.maximum((t + first) * per - 1, 0), 0)),
                  pl.BlockSpec((nb, FFN_HALO, d), lambda i, t: (i, jnp.minimum((t + first + 1) * per, last), 0)),
                  pl.BlockSpec((nb, 1, 6, d), lambda i, t: (i, jnp.minimum(t + first, 1), 0, 0)),
                  res((d, D_FF)), res((d, D_FF)), res((D_FF, d)),
                  pl.BlockSpec((3, D_FF), lambda i, t: (0, 0)), vec(D_FF), vec(d), vec(d)],
        out_specs=pl.BlockSpec((nb, TM, d), lambda i, t: (i, t, 0)),
        out_shape=jax.ShapeDtypeStruct((b, l - first * TM, d), F32),
        scratch_shapes=[pltpu.VMEM((nb * TM, D_FF), BF16),
                        pltpu.VMEM((FFN_GBUF, nb, TM + 2 * FFN_HALO, FF_CHUNK), F32)],
        compiler_params=_cparams(("parallel", "parallel")),
        name="conv_ffn",
    )(h, h, h, mod, wg_b, wu_b, wo_b, conv_w, conv_b.reshape(1, D_FF), ln_g.reshape(1, d), ln_b.reshape(1, d))


def _rope_tables(n):
    rows = n // GRID_W
    row = jnp.repeat(jnp.arange(rows), GRID_W)
    col = jnp.tile(jnp.arange(GRID_W), rows)
    n_freq = D_HEAD // 4
    inv = ROPE_BASE ** (-jnp.arange(n_freq, dtype=F32) / n_freq)
    ang = jnp.concatenate([row[:, None] * inv, col[:, None] * inv], axis=-1)
    ang = jnp.concatenate([jnp.zeros((CTX_LEN, D_HEAD // 2), F32), ang], axis=0)
    cos, sin = jnp.cos(ang), jnp.sin(ang)
    return jnp.concatenate([cos, cos], axis=-1), jnp.concatenate([-sin, sin], axis=-1)


def kernel(x, c, ctx, c_ctx, ada_w, ada_b, w_in, hgrn_lb_logits, hgrn_norm_w, ssm_conv_w, ssm_conv_b, ssm_dt_bias, ssm_a_log, ssm_d, ssm_norm_w, lru_conv_w, lru_conv_b, lru_wa, lru_ba, lru_wi, lru_bi, lru_lambda, ret_decay_logit, w_out, ln1_g, ln1_b, ffn_w_in, ffn_conv_w, ffn_conv_b, ffn_w_out, ln2_g, ln2_b):
    bsz, n, d = x.shape
    assert d == D_MODEL and ctx.shape[1] == CTX_LEN == TM and n % TM == 0 and n % GRID_W == 0
    nb = 2 if bsz % 2 == 0 else 1
    depth = ada_w.shape[0]

    cosf, sinf = _rope_tables(n)
    p_soft = jax.nn.softmax(hgrn_lb_logits.astype(F32), axis=0)
    lb_all = jnp.cumsum(p_soft, axis=0) - p_soft

    rows = -(-(bsz + 1) // 8) * 8
    cond = jnp.zeros((rows, d), F32).at[:bsz].set(c).at[bsz].set(c_ctx)
    mods = _ada_call(cond, ada_w, ada_b)

    n_in = w_in.shape[2]
    dt0 = COL_BXBC + XBC_W
    h = jnp.concatenate([ctx, x], axis=1)
    for l in range(depth):
        ml = mods[l].reshape(rows, 6, d)
        mod = jnp.stack([jnp.broadcast_to(ml[bsz][None], (bsz, 6, d)), ml[:bsz]], axis=1)
        w_l = w_in[l]
        w_perm = jnp.concatenate([w_l[:, :dt0], w_l[:, dt0 + 2 * B_HEADS:], w_l[:, dt0:dt0 + 2 * B_HEADS],
                                  jnp.zeros((d, W_COLS - n_in), F32)], axis=1).astype(BF16)
        conv_w = jnp.concatenate([ssm_conv_w[l], lru_conv_w[l]], axis=1)
        conv_b = jnp.concatenate([ssm_conv_b[l], lru_conv_b[l]]).reshape(1, CONV_HI - CONV_LO)
        p = _inproj_call(h, mod, w_perm, conv_w, conv_b, lb_all[l], cosf, sinf, nb)
        ya, yb, yc, yd = _mix_call(p, ssm_dt_bias[l], ssm_a_log[l], ssm_d[l],
                                   lru_wa[l], lru_ba[l], lru_wi[l], lru_bi[l], lru_lambda[l],
                                   ret_decay_logit[l].astype(F32), nb)
        h = _outproj_call(ya, yb, yc, yd, p, h, mod, w_out[l].astype(BF16), hgrn_norm_w[l], ssm_norm_w[l],
                          ln1_g[l], ln1_b[l], nb)
        wf = ffn_w_in[l]
        h = _ffn_call(h, mod, wf[:, :D_FF].astype(BF16), wf[:, D_FF:].astype(BF16),
                      ffn_w_out[l].astype(BF16), ffn_conv_w[l], ffn_conv_b[l], ln2_g[l], ln2_b[l],
                      nb, skip_ctx=(l == depth - 1))
    return h
```

```python
import functools
import math

import jax
import jax.numpy as jnp
from jax import lax
from jax.experimental import pallas as pl
from jax.experimental.pallas import tpu as pltpu

F32 = jnp.float32
BF16 = jnp.bfloat16

D_MODEL = 1024
DEPTH = 4
GRID_W = 64
CTX_LEN = 256
GROUP_W = 512
A_HEAD = 128
A_HEADS = 4
LB_FLOOR = 1e-30
B_HEADDIM = 64
B_HEADS = 8
B_GROUPS = 2
B_STATE = 128
XBC_W = GROUP_W + 2 * B_GROUPS * B_STATE
C_BLOCKS = 8
C_BLOCK = 64
C_POW = 8.0
D_HEAD = 128
D_HEADS = 4
ROPE_BASE = 10000.0
D_FF = 2816
ALPHA = (2 * DEPTH) ** 0.25
EPS = 1e-6
LOG2E = 1.4426950408889634
LN2 = 0.6931471805599453

TM = 256
FFN_HALO = 8
A_CHUNK = 64
A_SUB = 8
A_FBLK = 32
A_GUARD = 80.0
B_CHUNK = 128
D_CHUNK = 128
FF_CHUNK = 256
FFN_GBUF = 4
VMEM_LIMIT = 56 * 1024 * 1024

COL_AQ, COL_AI, COL_AF, COL_AG = 0, 512, 1024, 2048
COL_BZ, COL_BXBC = 2560, 3072
COL_CX, COL_CG = 4096, 4608
COL_DQ, COL_DK, COL_DV, COL_DG = 5120, 5632, 6144, 6656
W_DT = 7168
W_COLS = 7296
DT_W = 128
COL_AHI = 7168
COL_DT = 9216
P_W = 9344
CONV_LO, CONV_HI = COL_BXBC, COL_CG


def _cparams(sem):
    return pltpu.CompilerParams(dimension_semantics=sem, vmem_limit_bytes=VMEM_LIMIT)


def _softplus(x):
    return jnp.maximum(x, 0.0) + jnp.log2(1.0 + jnp.exp(-jnp.abs(x))) * LN2


def _sigmoid(x):
    return 1.0 / (1.0 + jnp.exp(-x))


def _silu(x):
    return x * _sigmoid(x)


def _gelu_tanh(x):
    return 0.5 * x * (1.0 + jnp.tanh(math.sqrt(2.0 / math.pi) * (x + 0.044715 * (x * x * x))))


def _dot(a, b):
    return jnp.dot(a, b, preferred_element_type=F32)


def _dot_nt(a, b):
    return lax.dot_general(a, b, (((1,), (1,)), ((), ())), preferred_element_type=F32)


def _split3(x):
    hi = x.astype(BF16)
    r1 = x - hi.astype(F32)
    mid = r1.astype(BF16)
    lo = (r1 - mid.astype(F32)).astype(BF16)
    return hi, mid, lo


def _sel_dot(m01, x):
    hi, mid, lo = _split3(x)
    return _dot(m01, hi) + _dot(m01, mid) + _dot(m01, lo)


_DONE = object()


def _seq_block(d, j, nblk):
    return jnp.where(d == 0, j, jnp.where(j == 0, 0, nblk - j))


def _ada_kernel(c_ref, w_ref, b_ref, o_ref):
    s = _silu(c_ref[...])
    o_ref[0] = jnp.dot(s, w_ref[0], preferred_element_type=F32,
                       precision=lax.Precision.HIGHEST) + b_ref[0]


def _ada_call(cond, ada_w, ada_b):
    depth, d, n6 = ada_w.shape
    rows = cond.shape[0]
    tn = 1536
    return pl.pallas_call(
        _ada_kernel,
        grid=(depth, n6 // tn),
        in_specs=[pl.BlockSpec((rows, d), lambda l, n: (0, 0)),
                  pl.BlockSpec((1, d, tn), lambda l, n: (l, 0, n)),
                  pl.BlockSpec((1, 1, tn), lambda l, n: (l, 0, n))],
        out_specs=pl.BlockSpec((1, rows, tn), lambda l, n: (l, 0, n)),
        out_shape=jax.ShapeDtypeStruct((depth, rows, n6), F32),
        compiler_params=_cparams(("parallel", "parallel")),
        name="ada_mod",
    )(cond, ada_w, ada_b.reshape(depth, 1, n6))


def _inproj_kernel(h_ref, hp_ref, hn_ref, mod_ref, w_ref, cw_ref, cb_ref, lb_ref, cos_ref, sin_ref, o_ref, g_ref,
                   *, nb, tn, nblk):
    rb = pl.program_id(1)
    prev_ok = rb >= 2
    next_ok = (rb >= 1) & (rb <= nblk - 2)
    m = mod_ref[:, 0]
    sc, sh = 1.0 + m[:, 1:2, :], m[:, 0:1, :]
    hrows = TM + 2 * FFN_HALO
    ub = (h_ref[...] * sc + sh).astype(BF16).reshape(nb * TM, D_MODEL)
    u_all = (jnp.concatenate([hp_ref[...], h_ref[...], hn_ref[...]], axis=1) * sc + sh).astype(BF16)
    u_all = u_all.reshape(nb * hrows, D_MODEL)
    r1 = lax.broadcasted_iota(jnp.int32, (1, hrows, 1), 1)
    keep = ((r1 >= FFN_HALO) | prev_ok) & ((r1 < FFN_HALO + TM) | next_ok)
    heavy = [COL_AF, COL_AF + GROUP_W, COL_DQ, COL_DK] + list(range(CONV_LO, CONV_HI, tn))
    plain = [n for n in range(0, W_COLS, tn) if n not in heavy]
    order = [n for pair in zip(heavy, plain) for n in pair] + plain[len(heavy):] + heavy[len(plain):]
    for n0 in order:
        w = min(tn, W_COLS - n0)
        res = _dot(u_all if CONV_LO <= n0 < CONV_HI else ub, w_ref[:, n0:n0 + w])
        if CONV_LO <= n0 < CONV_HI:
            cs = slice(n0 - CONV_LO, n0 - CONV_LO + tn)
            g = res.reshape(nb, hrows, tn)
            g_ref[...] = jnp.where(keep, g, 0.0)
            acc = cb_ref[:, cs].reshape(1, 1, tn)
            for jj in range(4):
                acc = acc + cw_ref[jj:jj + 1, cs].reshape(1, 1, tn) * g_ref[:, pl.ds(FFN_HALO - 2 + jj, TM), :]
            if n0 < COL_CX:
                acc = _silu(acc)
            o_ref[:, :, n0:n0 + tn] = acc.astype(BF16)
            continue
        if n0 in (COL_AF, COL_AF + GROUP_W):
            dd = (n0 - COL_AF) // GROUP_W
            lbv = lb_ref[dd:dd + 1, :]
            e = jnp.exp(-jnp.abs(res))
            big = 1.0 / (1.0 + e)
            small = e * big
            pos = res >= 0.0
            lf2 = jnp.log2(jnp.maximum(lbv, LB_FLOOR) + (1.0 - lbv) * jnp.where(pos, big, small))
            hi = lf2.astype(BF16)
            lo = (lf2 - hi.astype(F32)).astype(BF16)
            c_hi = COL_AHI + 2 * GROUP_W * dd
            o_ref[:, :, c_hi:c_hi + GROUP_W] = hi.reshape(nb, TM, GROUP_W)
            o_ref[:, :, c_hi + GROUP_W:c_hi + 2 * GROUP_W] = lo.reshape(nb, TM, GROUP_W)
            res = (1.0 - lbv) * jnp.where(pos, small, big)
        elif n0 in (COL_DQ, COL_DK):
            res = res.reshape(nb, TM, w)
            cosf, sinf = cos_ref[...], sin_ref[...]
            heads = []
            for hh in range(D_HEADS):
                samples = []
                for s in range(nb):
                    x = res[s, :, hh * D_HEAD:(hh + 1) * D_HEAD]
                    samples.append(x * cosf + pltpu.roll(x, D_HEAD // 2, 1) * sinf)
                heads.append(jnp.stack(samples))
            res = jnp.concatenate(heads, axis=2)
            if n0 == COL_DK:
                res = res * (D_HEAD ** -0.5)
        c_out = COL_DT if n0 == W_DT else n0
        o_ref[:, :, c_out:c_out + w] = res.astype(BF16).reshape(nb, TM, w)


def _inproj_call(h, mod, w_in_b, conv_w, conv_b, lb, cosf, sinf, nb):
    b, l, d = h.shape
    nblk = l // TM
    per = TM // FFN_HALO
    last = l // FFN_HALO - 1
    ncv = CONV_HI - CONV_LO
    kern = functools.partial(_inproj_kernel, nb=nb, tn=512, nblk=nblk)
    return pl.pallas_call(
        kern,
        grid=(b // nb, nblk),
        in_specs=[pl.BlockSpec((nb, TM, d), lambda i, t: (i, t, 0)),
                  pl.BlockSpec((nb, FFN_HALO, d), lambda i, t: (i, jnp.maximum(t * per - 1, 0), 0)),
                  pl.BlockSpec((nb, FFN_HALO, d), lambda i, t: (i, jnp.minimum((t + 1) * per, last), 0)),
                  pl.BlockSpec((nb, 1, 6, d), lambda i, t: (i, jnp.minimum(t, 1), 0, 0)),
                  pl.BlockSpec((d, W_COLS), lambda i, t: (0, 0), pipeline_mode=pl.Buffered(1)),
                  pl.BlockSpec((4, ncv), lambda i, t: (0, 0)),
                  pl.BlockSpec((1, ncv), lambda i, t: (0, 0)),
                  pl.BlockSpec((2, GROUP_W), lambda i, t: (0, 0)),
                  pl.BlockSpec((TM, D_HEAD), lambda i, t: (t, 0)),
                  pl.BlockSpec((TM, D_HEAD), lambda i, t: (t, 0))],
        out_specs=pl.BlockSpec((nb, TM, P_W), lambda i, t: (i, t, 0)),
        out_shape=jax.ShapeDtypeStruct((b, l, P_W), BF16),
        scratch_shapes=[pltpu.VMEM((nb, TM + 2 * FFN_HALO, 512), F32)],
        compiler_params=_cparams(("parallel", "parallel")),
        name="in_proj",
    )(h, h, h, mod, w_in_b, conv_w, conv_b, lb, cosf, sinf)


def _mixa_body(q_ref, v_ref, k_ref, hi_ref, lo_ref, y_ref, st_ref, sh_ref, rev):
    C, m = A_CHUNK, A_SUB
    nsl = C // m - 1

    row = lax.broadcasted_iota(jnp.int32, (C, C), 0)
    col = lax.broadcasted_iota(jnp.int32, (C, C), 1)
    tau_r = (C - 1 - row) if rev else row
    tau_c = (C - 1 - col) if rev else col
    blk_r, blk_c = tau_r // m, tau_c // m
    tri = (tau_c <= tau_r).astype(BF16)
    refrow = (C - m * blk_r) if rev else (m * blk_r - 1)
    rsel = ((col == refrow) & (blk_r >= 1)).astype(BF16)
    off_mask = blk_c < blk_r
    r1 = lax.broadcasted_iota(jnp.int32, (C, 1), 0)
    tau1 = (C - 1 - r1) if rev else r1
    blk1 = tau1 // m
    off1 = tau1 % m

    pad = m

    def chunk(i, carry):
        c = (TM // C - 1 - i) if rev else i
        sl = pl.ds(pl.multiple_of(c * C, C), C)
        q = q_ref[0, sl, :].astype(F32)
        v = v_ref[0, sl, :].astype(F32)
        k = k_ref[0, sl, :].astype(F32)
        log_f = (hi_ref[0, sl, :].astype(F32) + lo_ref[0, sl, :].astype(F32)) * (1.0 / LOG2E)
        b = _sel_dot(tri, log_f)
        rr = _sel_dot(rsel, b)
        qt = q * jnp.exp(jnp.minimum(b - rr, 0.0))
        sh_ref[0, pl.ds(pad, C), :] = k
        sh_ref[1, pl.ds(pad, C), :] = b
        sh_ref[2, pl.ds(pad, C), :] = v
        y_parts = []
        for h in range(A_HEADS):
            hs = slice(h * A_HEAD, (h + 1) * A_HEAD)
            qh, kh, vh, bh, qth = q[:, hs], k[:, hs], v[:, hs], b[:, hs], qt[:, hs]
            tot = bh[0:1, :] if rev else bh[C - 1:C, :]
            q_aug = jnp.concatenate(
                [jnp.where(blk1 == i, qth, 0.0) for i in range(1, nsl + 1)], axis=1).astype(BF16)
            k_parts = []
            for i in range(1, nsl + 1):
                rrow = (C - m * i) if rev else (m * i - 1)
                k_parts.append(kh * jnp.exp(jnp.minimum(bh[rrow:rrow + 1, :] - bh, 0.0)))
            k_aug = jnp.concatenate(k_parts, axis=1).astype(BF16)
            p_off = jnp.where(off_mask, _dot_nt(q_aug, k_aug), 0.0)
            y = _dot(p_off.astype(BF16), vh.astype(BF16))
            for dl in range(m):
                if dl == 0:
                    ks, bs, vs = kh, bh, vh
                else:
                    st = pad + dl if rev else pad - dl
                    ks = sh_ref[0, pl.ds(st, C), hs]
                    bs = sh_ref[1, pl.ds(st, C), hs]
                    vs = sh_ref[2, pl.ds(st, C), hs]
                tmp = qh * ks * jnp.exp(jnp.minimum(bh - bs, 0.0))
                p = jnp.sum(tmp, axis=-1, keepdims=True)
                y = y + jnp.where(off1 >= dl, p, 0.0) * vs
            st_t = st_ref[h]
            y = y + _dot_nt((qh * jnp.exp(bh)).astype(BF16), st_t.astype(BF16))
            kdec = (kh * jnp.exp(tot - bh)).astype(BF16)
            st_ref[h] = st_t * jnp.exp(tot) + _dot(vh.T.astype(BF16), kdec)
            y_parts.append(y)
        y_ref[0, 0, sl, :] = jnp.concatenate(y_parts, axis=1).astype(BF16)
        return carry

    lax.fori_loop(0, TM // C, chunk, 0)


def _mixa_tables(tm_ref, code_ref, rev):
    C, m = A_CHUNK, A_FBLK
    row = lax.broadcasted_iota(jnp.int32, (TM, TM), 0)
    col = lax.broadcasted_iota(jnp.int32, (TM, TM), 1)
    same_chunk = (row // C) == (col // C)
    tr, tc = row % C, col % C
    if rev:
        tr, tc = C - 1 - tr, C - 1 - tc
    same_blk = same_chunk & ((tr // m) == (tc // m))
    tm_ref[0] = (same_chunk & (tc <= tr)).astype(BF16)
    off =same_chunk & ((tc // m) < (tr // m))
    diag = same_blk & (tc <= tr)
    code_ref[...] = jnp.where(off, 1.0, jnp.where(diag, 2.0, 0.0))


def _mixa_fast(q_ref, v_ref, y_ref, st_ref, code_ref, k_ref, b_ref, bl_ref, bm_ref, rev):
    C, m = A_CHUNK, A_FBLK
    nch = TM // C
    nsl = C // m - 1
    code = code_ref[...]
    off_mask = code == 1.0
    diag_mask = code == 2.0
    r1 = lax.broadcasted_iota(jnp.int32, (TM, 1), 0) % C
    blk1 = ((C - 1 - r1) if rev else r1) // m
    rows = lambda a, c: a[c * C:(c + 1) * C, :]
    order = range(nch - 1, -1, -1) if rev else range(nch)
    y_parts = []
    for h in range(A_HEADS):
        hs = slice(h * A_HEAD, (h + 1) * A_HEAD)
        qh = q_ref[0, :, hs].astype(F32)
        vb = v_ref[0, :, hs]
        kh = k_ref[0, :, hs].astype(F32)
        bh, blh, bmh = b_ref[:, hs], bl_ref[:, hs], bm_ref[:, hs]
        qt = qh * jnp.exp2(blh)
        q_aug = jnp.concatenate(
            [jnp.where(blk1 == i, qt, 0.0) for i in range(1, nsl + 1)], axis=1).astype(BF16)
        k_parts = []
        for i in range(1, nsl + 1):
            rrow = (C - m * i) if rev else (m * i - 1)
            dif = jnp.concatenate([bh[c * C + rrow:c * C + rrow + 1, :] - rows(bh, c) for c in range(nch)], axis=0)
            k_parts.append(kh * jnp.exp2(jnp.minimum(dif, 0.0)))
        k_aug = jnp.concatenate(k_parts, axis=1).astype(BF16)
        yield
        p = jnp.where(off_mask, _dot_nt(q_aug, k_aug), 0.0)
        qd = (qh * jnp.exp2(bmh)).astype(BF16)
        kd = (kh * jnp.exp2(-bmh)).astype(BF16)
        p = p + jnp.where(diag_mask, _dot_nt(qd, kd), 0.0)
        yield
        y = _dot(p.astype(BF16), vb)
        yield
        qe = (qh * jnp.exp2(bh)).astype(BF16)
        tots = [bh[c * C:c * C + 1, :] if rev else bh[(c + 1) * C - 1:(c + 1) * C, :] for c in range(nch)]
        upd = [_dot(rows(vb, c).astype(F32).T.astype(BF16),
                    (rows(kh, c) * jnp.exp2(tots[c] - rows(bh, c))).astype(BF16)) for c in range(nch)]
        yield
        st = st_ref[h]
        y_inter = [None] * nch
        for c in order:
            y_inter[c] = _dot_nt(rows(qe, c), st.astype(BF16))
            st = st * jnp.exp2(tots[c]) + upd[c]
        st_ref[h] = st
        y_parts.append(y + jnp.concatenate(y_inter, axis=0))
        yield
    y_ref[0, 0] = jnp.concatenate(y_parts, axis=1).astype(BF16)


def _mixa_prepare(hi_ref, lo_ref, tm_ref, b_ref, bl_ref, bm_ref, result, rev):
    C, m = A_CHUNK, A_FBLK
    worst = jnp.zeros((1, 1), F32)
    lw = 2 * A_HEAD
    for h in range(GROUP_W // lw):
        hs = slice(h * lw, (h + 1) * lw)
        b = _dot(tm_ref[0], hi_ref[0, :, hs]) + _dot(tm_ref[0], lo_ref[0, :, hs])
        starts, mids = [], []
        for r0 in range(0, TM, m):
            pb = (r0 % C) // m
            first = (pb == C // m - 1) if rev else (pb == 0)
            prev_row = (r0 + m) if rev else (r0 - 1)
            mid_row = (r0 + m // 2) if rev else (r0 + m // 2 - 1)
            start = jnp.zeros((1, lw), F32) if first else b[prev_row:prev_row + 1, :]
            starts.append(jnp.broadcast_to(start, (m, lw)))
            mids.append(jnp.broadcast_to(b[mid_row:mid_row + 1, :], (m, lw)))
        bm = b - jnp.concatenate(mids, axis=0)
        b_ref[:, hs] = b
        bl_ref[:, hs] = b - jnp.concatenate(starts, axis=0)
        bm_ref[:, hs] = bm
        worst = jnp.maximum(worst, jnp.max(jnp.abs(bm), keepdims=True))
        yield
    result.append(worst[0, 0] <= A_GUARD * LOG2E)


def _mixb_body(xc_ref, dt_ref, dtb_ref, alog_ref, e_ref, dsk_ref, y_ref, st_ref, tri_ref, rev, d):
    C = B_CHUNK
    hpg = B_HEADS // B_GROUPS
    gw = hpg * B_HEADDIM
    row = lax.broadcasted_iota(jnp.int32, (C, C), 0)
    col = lax.broadcasted_iota(jnp.int32, (C, C), 1)
    trimask = (col >= row) if rev else (col <= row)
    tri = tri_ref[...]
    lane_head = lax.broadcasted_iota(jnp.int32, (1, gw), 1) // B_HEADDIM
    e01 = e_ref[0]
    dtb = dtb_ref[0]
    neg_a = -jnp.exp(alog_ref[0])

    def expand(x):
        hi = x.astype(BF16)
        lo = (x - hi.astype(F32)).astype(BF16)
        return _dot(hi, e01) + _dot(lo, e01)

    for c in (range(TM // C - 1, -1, -1) if rev else range(TM // C)):
        sl = pl.ds(c * C, C)
        bx = xc_ref[0, sl, 0:GROUP_W].astype(F32)
        bb = xc_ref[0, sl, GROUP_W:GROUP_W + B_GROUPS * B_STATE]
        bc = xc_ref[0, sl, GROUP_W + B_GROUPS * B_STATE:XBC_W]
        dt = _softplus(dt_ref[0, sl, :].astype(F32) + dtb)
        la = dt * neg_a
        b = _sel_dot(tri, la)
        b_exp = expand(b)
        dt_exp = _dot(dt.astype(BF16), e01)
        tot_exp = b_exp[0:1, :] if rev else b_exp[C - 1:C, :]
        v_all = bx * dt_exp
        vdec = v_all * jnp.exp(tot_exp - b_exp)
        eb = jnp.exp(b_exp)
        bt = b.T
        yield
        y_parts = []
        for g in range(B_GROUPS):
            gs = slice(g * gw, (g + 1) * gw)
            bg = bb[:, g * B_STATE:(g + 1) * B_STATE]
            cg = bc[:, g * B_STATE:(g + 1) * B_STATE]
            qk = _dot_nt(cg, bg)
            yield
            s_g = st_ref[g]
            vg = v_all[:, gs].astype(BF16)
            m_parts, v_parts = [], []
            for e in range(hpg):
                lane = 8 * d + g * hpg + e
                rel = jnp.where(trimask,
                                jnp.exp(jnp.minimum(b[:, lane:lane + 1] - bt[lane:lane + 1, :], 0.0)), 0.0)
                m_parts.append((qk * rel).astype(BF16))
                v_parts.append(jnp.where(lane_head == e, vg, jnp.zeros_like(vg)))
            y_g = (_dot(jnp.concatenate(m_parts, axis=1), jnp.concatenate(v_parts, axis=0))
                   + _dot(cg, s_g.astype(BF16)) * eb[:, gs])
            yield
            st_ref[g] = (s_g * jnp.exp(tot_exp[:, gs])
                         + _dot(bg.astype(F32).T.astype(BF16), vdec[:, gs].astype(BF16)))
            y_parts.append(y_g)
            yield
        y = jnp.concatenate(y_parts, axis=1)
        if not rev:
            y = y + dsk_ref[...] * bx
        y_ref[0, 0, sl, :] = y.astype(BF16)


def _mixc_body(y_ref, carry_ref, a_ref, b_ref, rev):
    grp = TM // 8
    a3 = a_ref[...].reshape(grp, 8, GROUP_W)
    b3 = b_ref[...].reshape(grp, 8, GROUP_W)
    ro = lax.broadcasted_iota(jnp.int32, (1, 8, 1), 1)
    for s in (1, 2, 4):
        valid = (ro + s <= 7) if rev else (ro >= s)
        shift = (8 - s) if rev else s
        a_sh = pltpu.roll(a3, shift, 1)
        b_sh = pltpu.roll(b3, shift, 1)
        b3 = b3 + a3 * jnp.where(valid, b_sh, 0.0)
        a3 = a3 * jnp.where(valid, a_sh, 1.0)
        yield
    cb = jnp.broadcast_to(carry_ref[...], (8, GROUP_W))
    last = 0 if rev else 7
    hs = [None] * grp
    for g in (range(grp - 1, -1, -1) if rev else range(grp)):
        hg = a3[g] * cb + b3[g]
        hs[g] = hg
        cb = jnp.broadcast_to(hg[last:last + 1, :], (8, GROUP_W))
        if g % 8 == 0:
            yield
    carry_ref[...] = cb[0:1, :]
    y_ref[0, 0] = jnp.concatenate(hs, axis=0).astype(BF16)


def _mixc_gates(x_ref, wa_ref, wi_ref, ba_ref, bi_ref, lam_ref, a_ref, b_ref):
    cxb = x_ref[0]
    cx = cxb.astype(F32)
    tile = 2 * A_HEAD
    bdot = lambda w_ref: jnp.concatenate(
        [_dot(cxb[:, t:t + tile], w_ref[0, t:t + tile, t:t + tile]) for t in range(0, GROUP_W, tile)], axis=1)
    r = _sigmoid(bdot(wa_ref) + ba_ref[0])
    gi = _sigmoid(bdot(wi_ref) + bi_ref[0])
    yield
    la = (-C_POW) * r * _softplus(-lam_ref[0])
    a = jnp.exp(la)
    a_ref[...] = a
    s = 1.0 - a * a
    root = jnp.where(s > 0.0, s * lax.rsqrt(s), 0.0)
    b_ref[...] = root * gi * cx
    yield


def _block_diag(w):
    nd, nb_, n, _ = w.shape
    eye = jnp.eye(nb_, dtype=w.dtype)
    return jnp.einsum("dhij,hg->dhigj", w, eye).reshape(nd, nb_ * n, nb_ * n)


def _mixd_body(q_ref, k_ref, v_ref, lgt_ref, y_ref, st_ref, rev):
    C = D_CHUNK
    lg = -_softplus(-lgt_ref[0])
    row = lax.broadcasted_iota(jnp.int32, (C, C), 0)
    col = lax.broadcasted_iota(jnp.int32, (C, C), 1)
    trimask = (col >= row) if rev else (col <= row)
    dist = jnp.maximum((col - row) if rev else (row - col), 0).astype(F32)
    r1 = lax.broadcasted_iota(jnp.int32, (C, 1), 0)
    tau1 = ((C - 1 - r1) if rev else r1).astype(F32)
    for c in (range(TM // C - 1, -1, -1) if rev else range(TM // C)):
        sl = pl.ds(c * C, C)
        y_parts = []
        for h in range(D_HEADS):
            hs = slice(h * D_HEAD, (h + 1) * D_HEAD)
            lgh = lg[h:h + 1, 0:1]
            qb = q_ref[0, sl, hs]
            kb = k_ref[0, sl, hs]
            vh = v_ref[0, sl, hs]
            kr = kb.astype(F32)
            qk = _dot_nt(qb, kb)
            yield
            rel = jnp.where(trimask, jnp.exp(lgh * dist), 0.0)
            s_h = st_ref[h]
            y = _dot((qk * rel).astype(BF16), vh) + _dot(qb, s_h.astype(BF16)) * jnp.exp(lgh * (tau1 + 1.0))
            kdec = kr * jnp.exp(lgh * (C - 1.0 - tau1))
            st_ref[h] = s_h * jnp.exp(lgh * float(C)) + _dot(kdec.T.astype(BF16), vh)
            y_parts.append(y)
            yield
        y_ref[0, 0, sl, :] = jnp.concatenate(y_parts, axis=1).astype(BF16)


def _mix_kernel(aq_ref, ai_ref, ak_ref, ahi_ref, alo_ref,
                bx_ref, bdt_ref, dtb_ref, alog_ref, e_ref, dsk_ref,
                cx_ref, wa_ref, wi_ref, ba_ref, bi_ref, lam_ref,
                dq_ref, dk_ref, dv_ref, lgt_ref,
                ya_ref, yb_ref, yc_ref, yd_ref,
                a_st, a_sh, a_tm, a_code, a_b, a_bl, a_bm, b_st, c_carry, c_a, c_b, d_st, a_bak, b_tri,
                *, nb):
    d = pl.program_id(1)
    j = pl.program_id(2)

    @pl.when(j == 0)
    def _():
        for r in (a_st, a_sh, b_st, c_carry, d_st):
            r[...] = jnp.zeros_like(r)

    def run(rev, dd):
        @pl.when(j == 0)
        def _():
            _mixa_tables(a_tm, a_code, rev)
            row = lax.broadcasted_iota(jnp.int32, (B_CHUNK, B_CHUNK), 0)
            col = lax.broadcasted_iota(jnp.int32, (B_CHUNK, B_CHUNK), 1)
            b_tri[...] = ((col >= row) if rev else (col <= row)).astype(BF16)

        a_bak[...] = a_st[...]
        oks = []
        streams = []
        for s in range(nb):
            one = lambda r, s=s: r.at[pl.ds(s, 1)]
            out = lambda r, s=s: r.at[:, pl.ds(s, 1)]

            def mixer_a(s=s, one=one, out=out):
                yield from _mixa_prepare(one(ahi_ref), one(alo_ref), a_tm, a_b.at[s], a_bl.at[s], a_bm.at[s], oks,
                                         rev)
                yield from _mixa_fast(one(aq_ref), one(ai_ref), out(ya_ref), a_st.at[s], a_code, one(ak_ref),
                                      a_b.at[s], a_bl.at[s], a_bm.at[s], rev)

            def mixer_c(s=s, one=one, out=out):
                yield from _mixc_gates(one(cx_ref), wa_ref, wi_ref, ba_ref, bi_ref, lam_ref, c_a.at[s], c_b.at[s])
                yield from _mixc_body(out(yc_ref), c_carry.at[s], c_a.at[s], c_b.at[s], rev)

            streams += [mixer_a(),
                        _mixb_body(one(bx_ref), one(bdt_ref), dtb_ref, alog_ref, e_ref, dsk_ref, out(yb_ref),
                                   b_st.at[s], b_tri, rev, dd),
                        mixer_c(),
                        _mixd_body(one(dq_ref), one(dk_ref), one(dv_ref), lgt_ref, out(yd_ref), d_st.at[s], rev)]
        while streams:
            for g in list(streams):
                if next(g, _DONE) is _DONE:
                    streams.remove(g)

        for s in range(nb):
            @pl.when(jnp.logical_not(oks[s]))
            def _():
                a_st[s] = a_bak[s]
                _mixa_body(aq_ref.at[pl.ds(s, 1)], ai_ref.at[pl.ds(s, 1)], ak_ref.at[pl.ds(s, 1)],
                           ahi_ref.at[pl.ds(s, 1)], alo_ref.at[pl.ds(s, 1)],
                           ya_ref.at[:, pl.ds(s, 1)], a_st.at[s], a_sh, rev)

    @pl.when(d == 0)
    def _():
        run(False, 0)

    @pl.when(d == 1)
    def _():
        run(True, 1)


def _mix_call(p, dt_bias, a_log, ssm_d, wa, ba, wi, bi, lam, decay_logit, nb):
    b, l, _ = p.shape
    nblk = l // TM
    rowblk = lambda d, j: _seq_block(d, j, nblk)
    pcol = lambda col, width=GROUP_W: pl.BlockSpec((nb, TM, width), lambda i, d, j: (i, rowblk(d, j), col // width))
    dircol = lambda col, step: pl.BlockSpec((nb, TM, GROUP_W),
                                            lambda i, d, j: (i, rowblk(d, j), col // GROUP_W + step * d))
    dirvec = lambda width: pl.BlockSpec((1, 1, width), lambda i, d, j: (d, 0, 0))
    pad_dir = lambda a: jnp.stack([jnp.pad(a[0], (0, DT_W - B_HEADS)),
                                   jnp.pad(a[1], (B_HEADS, DT_W - 2 * B_HEADS))]).reshape(2, 1, DT_W)
    lanes = jnp.arange(GROUP_W) // B_HEADDIM
    e01 = jnp.stack([(jnp.arange(DT_W)[:, None] == lanes[None, :] + 8 * dd) for dd in range(2)]).astype(BF16)
    dsk = jnp.repeat(ssm_d, B_HEADDIM).reshape(1, GROUP_W)
    vec = lambda a: a.reshape(2, 1, GROUP_W)
    wspec = pl.BlockSpec((1, GROUP_W, GROUP_W), lambda i, d, j: (d, 0, 0))
    lgt = jnp.broadcast_to(jnp.pad(decay_logit, ((0, 0), (0, 8 - D_HEADS)))[:, :, None], (2, 8, D_HEAD))
    yspec = pl.BlockSpec((1, nb, TM, GROUP_W), lambda i, d, j: (d, i, rowblk(d, j), 0))
    yshape = jax.ShapeDtypeStruct((2, b, l, GROUP_W), BF16)
    act = lambda: pltpu.VMEM((nb, TM, GROUP_W), F32)
    return pl.pallas_call(
        functools.partial(_mix_kernel, nb=nb),
        grid=(b // nb, 2, nblk),
        in_specs=[pcol(COL_AQ), pcol(COL_AI), dircol(COL_AF, 1), dircol(COL_AHI, 2), dircol(COL_AHI + GROUP_W, 2),
                  pcol(COL_BXBC, XBC_W), pcol(COL_DT, DT_W), dirvec(DT_W), dirvec(DT_W),
                  pl.BlockSpec((1, DT_W, GROUP_W), lambda i, d, j: (d, 0, 0)),
                  pl.BlockSpec((1, GROUP_W), lambda i, d, j: (0, 0)),
                  pcol(COL_CX), wspec, wspec, dirvec(GROUP_W), dirvec(GROUP_W), dirvec(GROUP_W),
                  pcol(COL_DQ), pcol(COL_DK), pcol(COL_DV),
                  pl.BlockSpec((1, 8, D_HEAD), lambda i, d, j: (d, 0, 0))],
        out_specs=[yspec, yspec, yspec, yspec],
        out_shape=[yshape, yshape, yshape, yshape],
        scratch_shapes=[pltpu.VMEM((nb, A_HEADS, A_HEAD, A_HEAD), F32),
                        pltpu.VMEM((3, A_CHUNK + 2 * A_SUB, GROUP_W), F32),
                        pltpu.VMEM((1, TM, TM), BF16),
                        pltpu.VMEM((TM, TM), F32),
                        act(), act(), act(),
                        pltpu.VMEM((nb, B_GROUPS, B_STATE, GROUP_W // B_GROUPS), F32),
                        pltpu.VMEM((nb, 1, GROUP_W), F32), act(), act(),
                        pltpu.VMEM((nb, D_HEADS, D_HEAD, D_HEAD), F32),
                        pltpu.VMEM((nb, A_HEADS, A_HEAD, A_HEAD), F32),
                        pltpu.VMEM((B_CHUNK, B_CHUNK), BF16)],
        compiler_params=_cparams(("parallel", "parallel", "arbitrary")),
        name="mixers",
    )(p, p, p, p, p,
      p, p, pad_dir(dt_bias), pad_dir(a_log), e01, dsk,
      p, _block_diag(wa).astype(BF16), _block_diag(wi).astype(BF16), vec(ba), vec(bi), vec(lam),
      p, p, p, lgt)


def _layer_norm_rows(x, g, b):
    mu = jnp.mean(x, axis=-1, keepdims=True)
    xc = x - mu
    var = jnp.mean(xc * xc, axis=-1, keepdims=True)
    return xc * lax.rsqrt(var + EPS) * g + b


def _outproj_kernel(yaf, yab, ybf, ybb, ycf, ycb, ydf, ydb, ag_ref, bz_ref, cg_ref, dg_ref,
                    h_ref, mod_ref, w_ref, nwa_ref, nwb_ref, lg_ref, lb_ref, o_ref, *, nb):
    rows = nb * TM
    ld = lambda f, bk: (f[0].astype(F32) + bk[0].astype(F32)).reshape(rows, GROUP_W)
    gate = lambda r: r[...].astype(F32).reshape(rows, GROUP_W)
    ya = ld(yaf, yab)
    parts = []
    for h in range(A_HEADS):
        x = ya[:, h * A_HEAD:(h + 1) * A_HEAD]
        parts.append(x * lax.rsqrt(jnp.mean(x * x, axis=-1, keepdims=True) + EPS))
    oa = jnp.concatenate(parts, axis=1) * nwa_ref[...] * _silu(gate(ag_ref))
    yb = ld(ybf, ybb) * _silu(gate(bz_ref))
    gw = GROUP_W // B_GROUPS
    parts = []
    for g in range(B_GROUPS):
        x = yb[:, g * gw:(g + 1) * gw]
        parts.append(x * lax.rsqrt(jnp.mean(x * x, axis=-1, keepdims=True) + EPS))
    ob = jnp.concatenate(parts, axis=1) * nwb_ref[...]
    oc = ld(ycf, ycb) * _gelu_tanh(gate(cg_ref))
    yd = ld(ydf, ydb)
    parts = []
    for h in range(D_HEADS):
        x = yd[:, h * D_HEAD:(h + 1) * D_HEAD]
        mu = jnp.mean(x, axis=-1, keepdims=True)
        xc = x - mu
        parts.append(xc * lax.rsqrt(jnp.mean(xc * xc, axis=-1, keepdims=True) + EPS))
    od = jnp.concatenate(parts, axis=1) * _silu(gate(dg_ref))
    mix = (_dot(oa.astype(BF16), w_ref[0:GROUP_W, :])
           + _dot(ob.astype(BF16), w_ref[GROUP_W:2 * GROUP_W, :])
           + _dot(oc.astype(BF16), w_ref[2 * GROUP_W:3 * GROUP_W, :])
           + _dot(od.astype(BF16), w_ref[3 * GROUP_W:4 * GROUP_W, :]))
    m = mod_ref[:, 0]
    hn = ALPHA * h_ref[...] + m[:, 2:3, :] * mix.reshape(nb, TM, D_MODEL)
    o_ref[...] = _layer_norm_rows(hn, lg_ref[...], lb_ref[...])


def _outproj_call(ya, yb, yc, yd, p, h, mod, w_out_b, nwa, nwb, ln_g, ln_b, nb):
    b, l, d = h.shape
    ydir = lambda dd: pl.BlockSpec((1, nb, TM, GROUP_W), lambda i, t: (dd, i, t, 0))
    pcol = lambda col: pl.BlockSpec((nb, TM, GROUP_W), lambda i, t: (i, t, col // GROUP_W))
    vec = lambda n: pl.BlockSpec((1, n), lambda i, t: (0, 0))
    return pl.pallas_call(
        functools.partial(_outproj_kernel, nb=nb),
        grid=(b // nb, l // TM),
        in_specs=[ydir(0), ydir(1), ydir(0), ydir(1), ydir(0), ydir(1), ydir(0), ydir(1),
                  pcol(COL_AG), pcol(COL_BZ), pcol(COL_CG), pcol(COL_DG),
                  pl.BlockSpec((nb, TM, d), lambda i, t: (i, t, 0)),
                  pl.BlockSpec((nb, 1, 6, d), lambda i, t: (i, jnp.minimum(t, 1), 0, 0)),
                  pl.BlockSpec((4 * GROUP_W, d), lambda i, t: (0, 0), pipeline_mode=pl.Buffered(1)),
                  vec(GROUP_W), vec(GROUP_W), vec(d), vec(d)],
        out_specs=pl.BlockSpec((nb, TM, d), lambda i, t: (i, t, 0)),
        out_shape=jax.ShapeDtypeStruct((b, l, d), F32),
        compiler_params=_cparams(("parallel", "parallel")),
        name="out_proj",
    )(ya, ya, yb, yb, yc, yc, yd, yd, p, p, p, p, h, mod, w_out_b,
      nwa.reshape(1, GROUP_W), nwb.reshape(1, GROUP_W), ln_g.reshape(1, d), ln_b.reshape(1, d))


def _ffn_kernel(h_ref, hp_ref, hn_ref, mod_ref, wg_ref, wu_ref, wo_ref, cw_ref, cb_ref, lg_ref, lb_ref,
                o_ref, act_ref, g_ref, *, nb, nblk, first_blk):
    rb = pl.program_id(1) + first_blk
    prev_ok = rb >= 2
    next_ok = (rb >= 1) & (rb <= nblk - 2)
    m = mod_ref[:, 0]
    sh, sc = m[:, 3:4, :], m[:, 4:5, :]
    hm = h_ref[...]
    hrows = TM + 2 * FFN_HALO
    x_all = jnp.concatenate([hp_ref[...], hm, hn_ref[...]], axis=1) * (1.0 + sc) + sh
    xb_all = x_all.astype(BF16).reshape(nb * hrows, D_MODEL)
    xb = x_all[:, FFN_HALO:FFN_HALO + TM, :].astype(BF16).reshape(nb * TM, D_MODEL)
    r1 = lax.broadcasted_iota(jnp.int32, (1, hrows, 1), 1)
    keep = ((r1 >= FFN_HALO) | prev_ok) & ((r1 < FFN_HALO + TM) | next_ok)
    for c0 in range(0, D_FF, FF_CHUNK):
        cs = slice(c0, c0 + FF_CHUNK)
        g = _dot(xb_all, wg_ref[:, cs]).reshape(nb, hrows, FF_CHUNK)
        up = _dot(xb, wu_ref[:, cs])
        gi = (c0 // FF_CHUNK) % FFN_GBUF
        g_ref[gi] = jnp.where(keep, g, 0.0)
        gc = cb_ref[:, cs].reshape(1, 1, FF_CHUNK)
        for jj in range(3):
            gc = gc + (cw_ref[jj:jj + 1, cs].reshape(1, 1, FF_CHUNK)
                       * g_ref[gi, :, pl.ds(FFN_HALO - 1 + jj, TM), :])
        act_ref[:, cs] =(_gelu_tanh(gc).reshape(nb * TM, FF_CHUNK) * up).astype(BF16)
    f = _dot(act_ref[...], wo_ref[...])
    hn2 = ALPHA * hm + m[:, 5:6, :] * f.reshape(nb, TM, D_MODEL)
    o_ref[...] = _layer_norm_rows(hn2, lg_ref[...], lb_ref[...])


def _ffn_call(h, mod, wg_b, wu_b, wo_b, conv_w, conv_b, ln_g, ln_b, nb, skip_ctx):
    b, l, d = h.shape
    nblk = l // TM
    first = 1 if skip_ctx else 0
    per = TM // FFN_HALO
    last = l // FFN_HALO - 1
    vec = lambda n: pl.BlockSpec((1, n), lambda i, t: (0, 0))
    res = lambda shape: pl.BlockSpec(shape, lambda i, t: (0, 0), pipeline_mode=pl.Buffered(1))
    return pl.pallas_call(
        functools.partial(_ffn_kernel, nb=nb, nblk=nblk, first_blk=first),
        grid=(b // nb, nblk - first),
        in_specs=[pl.BlockSpec((nb, TM, d), lambda i, t: (i, t + first, 0)),
                  pl.BlockSpec((nb, FFN_HALO, d), lambda i, t: (i, jnp.maximum((t + first) * per - 1, 0), 0)),
                  pl.BlockSpec((nb, FFN_HALO, d), lambda i, t: (i, jnp.minimum((t + first + 1) * per, last), 0)),
                  pl.BlockSpec((nb, 1, 6, d), lambda i, t: (i, jnp.minimum(t + first, 1), 0, 0)),
                  res((d, D_FF)), res((d, D_FF)), res((D_FF, d)),
                  pl.BlockSpec((3, D_FF), lambda i, t: (0, 0)), vec(D_FF), vec(d), vec(d)],
        out_specs=pl.BlockSpec((nb, TM, d), lambda i, t: (i, t, 0)),
        out_shape=jax.ShapeDtypeStruct((b, l - first * TM, d), F32),
        scratch_shapes=[pltpu.VMEM((nb * TM, D_FF), BF16),
                        pltpu.VMEM((FFN_GBUF, nb, TM + 2 * FFN_HALO, FF_CHUNK), F32)],
        compiler_params=_cparams(("parallel", "parallel")),
        name="conv_ffn",
    )(h, h, h, mod, wg_b, wu_b, wo_b, conv_w, conv_b.reshape(1, D_FF), ln_g.reshape(1, d), ln_b.reshape(1, d))


def _rope_tables(n):
    rows = n // GRID_W
    row = jnp.repeat(jnp.arange(rows), GRID_W)
    col = jnp.tile(jnp.arange(GRID_W), rows)
    n_freq = D_HEAD // 4
    inv = ROPE_BASE ** (-jnp.arange(n_freq, dtype=F32) / n_freq)
    ang = jnp.concatenate([row[:, None] * inv, col[:, None] * inv], axis=-1)
    ang = jnp.concatenate([jnp.zeros((CTX_LEN, D_HEAD // 2), F32), ang], axis=0)
    cos, sin = jnp.cos(ang), jnp.sin(ang)
    return jnp.concatenate([cos, cos], axis=-1), jnp.concatenate([-sin, sin], axis=-1)


def kernel(x, c, ctx, c_ctx, ada_w, ada_b, w_in, hgrn_lb_logits, hgrn_norm_w, ssm_conv_w, ssm_conv_b, ssm_dt_bias, ssm_a_log, ssm_d, ssm_norm_w, lru_conv_w, lru_conv_b, lru_wa, lru_ba, lru_wi, lru_bi, lru_lambda, ret_decay_logit, w_out, ln1_g, ln1_b, ffn_w_in, ffn_conv_w, ffn_conv_b, ffn_w_out, ln2_g, ln2_b):
    bsz, n, d = x.shape
    assert d == D_MODEL and ctx.shape[1] == CTX_LEN == TM and n % TM == 0 and n % GRID_W == 0
    nb = 2 if bsz % 2 == 0 else 1
    depth = ada_w.shape[0]

    cosf, sinf = _rope_tables(n)
    p_soft = jax.nn.softmax(hgrn_lb_logits.astype(F32), axis=0)
    lb_all = jnp.cumsum(p_soft, axis=0) - p_soft

    rows = -(-(bsz + 1) // 8) * 8
    cond = jnp.zeros((rows, d), F32).at[:bsz].set(c).at[bsz].set(c_ctx)
    mods = _ada_call(cond, ada_w, ada_b)

    n_in = w_in.shape[2]
    dt0 = COL_BXBC + XBC_W
    h = jnp.concatenate([ctx, x], axis=1)
    for l in range(depth):
        ml = mods[l].reshape(rows, 6, d)
        mod = jnp.stack([jnp.broadcast_to(ml[bsz][None], (bsz, 6, d)), ml[:bsz]], axis=1)
        w_l = w_in[l]
        w_perm = jnp.concatenate([w_l[:, :dt0], w_l[:, dt0 + 2 * B_HEADS:], w_l[:, dt0:dt0 + 2 * B_HEADS],
                                  jnp.zeros((d, W_COLS - n_in), F32)], axis=1).astype(BF16)
        conv_w = jnp.concatenate([ssm_conv_w[l], lru_conv_w[l]], axis=1)
        conv_b = jnp.concatenate([ssm_conv_b[l], lru_conv_b[l]]).reshape(1, CONV_HI - CONV_LO)
        p = _inproj_call(h, mod, w_perm, conv_w, conv_b, lb_all[l], cosf, sinf, nb)
        ya, yb, yc, yd = _mix_call(p, ssm_dt_bias[l], ssm_a_log[l], ssm_d[l],
                                   lru_wa[l], lru_ba[l], lru_wi[l], lru_bi[l], lru_lambda[l],
                                   ret_decay_logit[l].astype(F32), nb)
        h = _outproj_call(ya, yb, yc, yd, p, h, mod, w_out[l].astype(BF16), hgrn_norm_w[l], ssm_norm_w[l],
                          ln1_g[l], ln1_b[l], nb)
        wf = ffn_w_in[l]
        h = _ffn_call(h, mod, wf[:, :D_FF].astype(BF16), wf[:, D_FF:].astype(BF16),
                      ffn_w_out[l].astype(BF16), ffn_conv_w[l], ffn_conv_b[l], ln2_g[l], ln2_b[l],
                      nb, skip_ctx=(l == depth - 1))
    return h
```

```python
import functools
import math

import jax
import jax.numpy as jnp
from jax import lax
from jax.experimental import pallas as pl
from jax.experimental.pallas import tpu as pltpu

F32 = jnp.float32
BF16 = jnp.bfloat16

D_MODEL = 1024
DEPTH = 4
GRID_W = 64
CTX_LEN = 256
GROUP_W = 512
A_HEAD = 128
A_HEADS = 4
LB_FLOOR = 1e-30
B_HEADDIM = 64
B_HEADS = 8
B_GROUPS = 2
B_STATE = 128
XBC_W = GROUP_W + 2 * B_GROUPS * B_STATE
C_BLOCKS = 8
C_BLOCK = 64
C_POW = 8.0
D_HEAD = 128
D_HEADS = 4
ROPE_BASE = 10000.0
D_FF = 2816
ALPHA = (2 * DEPTH) ** 0.25
EPS = 1e-6
LOG2E = 1.4426950408889634
LN2 = 0.6931471805599453

TM = 256
FFN_HALO = 8
A_CHUNK = 64
A_SUB = 8
A_FBLK = 32
A_GUARD = 80.0
B_CHUNK = 128
D_CHUNK = 128
FF_CHUNK = 256
FFN_GBUF = 4
VMEM_LIMIT = 56 * 1024 * 1024

COL_AQ, COL_AI, COL_AF, COL_AG = 0, 512, 1024, 2048
COL_BZ, COL_BXBC = 2560, 3072
COL_CX, COL_CG = 4096, 4608
COL_DQ, COL_DK, COL_DV, COL_DG = 5120, 5632, 6144, 6656
W_DT = 7168
W_COLS = 7296
DT_W = 128
COL_AHI = 7168
COL_DT = 9216
P_W = 9344
CONV_LO, CONV_HI = COL_BXBC, COL_CG


def _cparams(sem):
    return pltpu.CompilerParams(dimension_semantics=sem, vmem_limit_bytes=VMEM_LIMIT)


def _softplus(x):
    return jnp.maximum(x, 0.0) + jnp.log2(1.0 + jnp.exp(-jnp.abs(x))) * LN2


def _sigmoid(x):
    return 1.0 / (1.0 + jnp.exp(-x))


def _silu(x):
    return x * _sigmoid(x)


def _gelu_tanh(x):
    return 0.5 * x * (1.0 + jnp.tanh(math.sqrt(2.0 / math.pi) * (x + 0.044715 * (x * x * x))))


def _dot(a, b):
    return jnp.dot(a, b, preferred_element_type=F32)


def _dot_nt(a, b):
    return lax.dot_general(a, b, (((1,), (1,)), ((), ())), preferred_element_type=F32)


def _split3(x):
    hi = x.astype(BF16)
    r1 = x - hi.astype(F32)
    mid = r1.astype(BF16)
    lo = (r1 - mid.astype(F32)).astype(BF16)
    return hi, mid, lo


def _sel_dot(m01, x):
    hi, mid, lo = _split3(x)
    return _dot(m01, hi) + _dot(m01, mid) + _dot(m01, lo)


_DONE = object()


def _seq_block(d, j, nblk):
    return jnp.where(d == 0, j, jnp.where(j == 0, 0, nblk - j))


def _ada_kernel(c_ref, w_ref, b_ref, o_ref):
    s = _silu(c_ref[...])
    o_ref[0] = jnp.dot(s, w_ref[0], preferred_element_type=F32,
                       precision=lax.Precision.HIGHEST) + b_ref[0]


def _ada_call(cond, ada_w, ada_b):
    depth, d, n6 = ada_w.shape
    rows = cond.shape[0]
    tn = 1536
    return pl.pallas_call(
        _ada_kernel,
        grid=(depth, n6 // tn),
        in_specs=[pl.BlockSpec((rows, d), lambda l, n: (0, 0)),
                  pl.BlockSpec((1, d, tn), lambda l, n: (l, 0, n)),
                  pl.BlockSpec((1, 1, tn), lambda l, n: (l, 0, n))],
        out_specs=pl.BlockSpec((1, rows, tn), lambda l, n: (l, 0, n)),
        out_shape=jax.ShapeDtypeStruct((depth, rows, n6), F32),
        compiler_params=_cparams(("parallel", "parallel")),
        name="ada_mod",
    )(cond, ada_w, ada_b.reshape(depth, 1, n6))


def _inproj_kernel(h_ref, hp_ref, hn_ref, mod_ref, w_ref, cw_ref, cb_ref, lb_ref, cos_ref, sin_ref, o_ref, g_ref,
                   *, nb, tn, nblk):
    rb = pl.program_id(1)
    prev_ok = rb >= 2
    next_ok = (rb >= 1) & (rb <= nblk - 2)
    m = mod_ref[:, 0]
    sc, sh = 1.0 + m[:, 1:2, :], m[:, 0:1, :]
    hrows = TM + 2 * FFN_HALO
    um = h_ref[...] * sc + sh
    ub = um.astype(BF16).reshape(nb * TM, D_MODEL)
    u_all = jnp.concatenate([hp_ref[...] * sc + sh, um, hn_ref[...] * sc + sh], axis=1).astype(BF16)
    u_all = u_all.reshape(nb * hrows, D_MODEL)
    r1 = lax.broadcasted_iota(jnp.int32, (1, hrows, 1), 1)
    keep = ((r1 >= FFN_HALO) | prev_ok) & ((r1 < FFN_HALO + TM) | next_ok)
    heavy = [COL_AF, COL_AF + GROUP_W, COL_DQ, COL_DK] + list(range(CONV_LO, CONV_HI, tn))
    plain = [n for n in range(0, W_COLS, tn) if n not in heavy]
    order = [n for pair in zip(heavy, plain) for n in pair] + plain[len(heavy):] + heavy[len(plain):]
    for n0 in order:
        w = min(tn, W_COLS - n0)
        res = _dot(u_all if CONV_LO <= n0 < CONV_HI else ub, w_ref[:, n0:n0 + w])
        if CONV_LO <= n0 < CONV_HI:
            cs = slice(n0 - CONV_LO, n0 - CONV_LO + tn)
            g = res.reshape(nb, hrows, tn)
            g_ref[...] = jnp.where(keep, g, 0.0)
            acc = cb_ref[:, cs].reshape(1, 1, tn)
            for jj in range(4):
                acc = acc + cw_ref[jj:jj + 1, cs].reshape(1, 1, tn) * g_ref[:, pl.ds(FFN_HALO - 2 + jj, TM), :]
            if n0 < COL_CX:
                acc = _silu(acc)
            o_ref[:, :, n0:n0 + tn] = acc.astype(BF16)
            continue
        if n0 in (COL_AF, COL_AF + GROUP_W):
            dd = (n0 - COL_AF) // GROUP_W
            lbv = lb_ref[dd:dd + 1, :]
            e = jnp.exp(-jnp.abs(res))
            big = 1.0 / (1.0 + e)
            small = e * big
            pos = res >= 0.0
            lf2 = jnp.log2(jnp.maximum(lbv, LB_FLOOR) + (1.0 - lbv) * jnp.where(pos, big, small))
            hi = lf2.astype(BF16)
            lo = (lf2 - hi.astype(F32)).astype(BF16)
            c_hi = COL_AHI + 2 * GROUP_W * dd
            o_ref[:, :, c_hi:c_hi + GROUP_W] = hi.reshape(nb, TM, GROUP_W)
            o_ref[:, :, c_hi + GROUP_W:c_hi + 2 * GROUP_W] = lo.reshape(nb, TM, GROUP_W)
            res = (1.0 - lbv) * jnp.where(pos, small, big)
        elif n0 in (COL_DQ, COL_DK):
            res = res.reshape(nb, TM, w)
            cosf, sinf = cos_ref[...], sin_ref[...]
            heads = []
            for hh in range(D_HEADS):
                samples = []
                for s in range(nb):
                    x = res[s, :, hh * D_HEAD:(hh + 1) * D_HEAD]
                    samples.append(x * cosf + pltpu.roll(x, D_HEAD // 2, 1) * sinf)
                heads.append(jnp.stack(samples))
            res = jnp.concatenate(heads, axis=2)
            if n0 == COL_DK:
                res = res * (D_HEAD ** -0.5)
        c_out = COL_DT if n0 == W_DT else n0
        o_ref[:, :, c_out:c_out + w] = res.astype(BF16).reshape(nb, TM, w)


def _inproj_call(h, mod, w_in_b, conv_w, conv_b, lb, cosf, sinf, nb):
    b, l, d = h.shape
    nblk = l // TM
    per = TM // FFN_HALO
    last = l // FFN_HALO - 1
    ncv = CONV_HI - CONV_LO
    kern = functools.partial(_inproj_kernel, nb=nb, tn=512, nblk=nblk)
    return pl.pallas_call(
        kern,
        grid=(b // nb, nblk),
        in_specs=[pl.BlockSpec((nb, TM, d), lambda i, t: (i, t, 0)),
                  pl.BlockSpec((nb, FFN_HALO, d), lambda i, t: (i, jnp.maximum(t * per - 1, 0), 0)),
                  pl.BlockSpec((nb, FFN_HALO, d), lambda i, t: (i, jnp.minimum((t + 1) * per, last), 0)),
                  pl.BlockSpec((nb, 1, 6, d), lambda i, t: (i, jnp.minimum(t, 1), 0, 0)),
                  pl.BlockSpec((d, W_COLS), lambda i, t: (0, 0), pipeline_mode=pl.Buffered(1)),
                  pl.BlockSpec((4, ncv), lambda i, t: (0, 0)),
                  pl.BlockSpec((1, ncv), lambda i, t: (0, 0)),
                  pl.BlockSpec((2, GROUP_W), lambda i, t: (0, 0)),
                  pl.BlockSpec((TM, D_HEAD), lambda i, t: (t, 0)),
                  pl.BlockSpec((TM, D_HEAD), lambda i, t: (t, 0))],
        out_specs=pl.BlockSpec((nb, TM, P_W), lambda i, t: (i, t, 0)),
        out_shape=jax.ShapeDtypeStruct((b, l, P_W), BF16),
        scratch_shapes=[pltpu.VMEM((nb, TM + 2 * FFN_HALO, 512), F32)],
        compiler_params=_cparams(("parallel", "parallel")),
        name="in_proj",
    )(h, h, h, mod, w_in_b, conv_w, conv_b, lb, cosf, sinf)


def _mixa_body(q_ref, v_ref, k_ref, hi_ref, lo_ref, y_ref, st_ref, sh_ref, rev):
    C, m = A_CHUNK, A_SUB
    nsl = C // m - 1

    row = lax.broadcasted_iota(jnp.int32, (C, C), 0)
    col = lax.broadcasted_iota(jnp.int32, (C, C), 1)
    tau_r = (C - 1 - row) if rev else row
    tau_c = (C - 1 - col) if rev else col
    blk_r, blk_c = tau_r // m, tau_c // m
    tri = (tau_c <= tau_r).astype(BF16)
    refrow = (C - m * blk_r) if rev else (m * blk_r - 1)
    rsel = ((col == refrow) & (blk_r >= 1)).astype(BF16)
    off_mask = blk_c < blk_r
    r1 = lax.broadcasted_iota(jnp.int32, (C, 1), 0)
    tau1 = (C - 1 - r1) if rev else r1
    blk1 = tau1 // m
    off1 = tau1 % m

    pad = m

    def chunk(i, carry):
        c = (TM // C - 1 - i) if rev else i
        sl = pl.ds(pl.multiple_of(c * C, C), C)
        q = q_ref[0, sl, :].astype(F32)
        v = v_ref[0, sl, :].astype(F32)
        k = k_ref[0, sl, :].astype(F32)
        log_f = (hi_ref[0, sl, :].astype(F32) + lo_ref[0, sl, :].astype(F32)) * (1.0 / LOG2E)
        b = _sel_dot(tri, log_f)
        rr = _sel_dot(rsel, b)
        qt = q * jnp.exp(jnp.minimum(b - rr, 0.0))
        sh_ref[0, pl.ds(pad, C), :] = k
        sh_ref[1, pl.ds(pad, C), :] = b
        sh_ref[2, pl.ds(pad, C), :] = v
        y_parts = []
        for h in range(A_HEADS):
            hs = slice(h * A_HEAD, (h + 1) * A_HEAD)
            qh, kh, vh, bh, qth = q[:, hs], k[:, hs], v[:, hs], b[:, hs], qt[:, hs]
            tot = bh[0:1, :] if rev else bh[C - 1:C, :]
            q_aug = jnp.concatenate(
                [jnp.where(blk1 == i, qth, 0.0) for i in range(1, nsl + 1)], axis=1).astype(BF16)
            k_parts = []
            for i in range(1, nsl + 1):
                rrow = (C - m * i) if rev else (m * i - 1)
                k_parts.append(kh * jnp.exp(jnp.minimum(bh[rrow:rrow + 1, :] - bh, 0.0)))
            k_aug = jnp.concatenate(k_parts, axis=1).astype(BF16)
            p_off = jnp.where(off_mask, _dot_nt(q_aug, k_aug), 0.0)
            y = _dot(p_off.astype(BF16), vh.astype(BF16))
            for dl in range(m):
                if dl == 0:
                    ks, bs, vs = kh, bh, vh
                else:
                    st = pad + dl if rev else pad - dl
                    ks = sh_ref[0, pl.ds(st, C), hs]
                    bs = sh_ref[1, pl.ds(st, C), hs]
                    vs = sh_ref[2, pl.ds(st, C), hs]
                tmp = qh * ks * jnp.exp(jnp.minimum(bh - bs, 0.0))
                p = jnp.sum(tmp, axis=-1, keepdims=True)
                y = y + jnp.where(off1 >= dl, p, 0.0) * vs
            st_t = st_ref[h]
            y = y + _dot_nt((qh * jnp.exp(bh)).astype(BF16), st_t.astype(BF16))
            kdec = (kh * jnp.exp(tot - bh)).astype(BF16)
            st_ref[h] = st_t * jnp.exp(tot) + _dot(vh.T.astype(BF16), kdec)
            y_parts.append(y)
        y_ref[0, 0, sl, :] = jnp.concatenate(y_parts, axis=1).astype(BF16)
        return carry

    lax.fori_loop(0, TM // C, chunk, 0)


def _mixa_tables(tm_ref, code_ref, rev):
    C, m = A_CHUNK, A_FBLK
    row = lax.broadcasted_iota(jnp.int32, (TM, TM), 0)
    col = lax.broadcasted_iota(jnp.int32, (TM, TM), 1)
    same_chunk = (row // C) == (col // C)
    tr, tc = row % C, col % C
    if rev:
        tr, tc = C - 1 - tr, C - 1 - tc
    same_blk = same_chunk & ((tr // m) == (tc // m))
    tm_ref[0] = (same_chunk & (tc <= tr)).astype(BF16)
    off =same_chunk & ((tc // m) < (tr // m))
    diag = same_blk & (tc <= tr)
    code_ref[...] = jnp.where(off, 1.0, jnp.where(diag, 2.0, 0.0))


def _mixa_fast(q_ref, v_ref, y_ref, st_ref, code_ref, k_ref, b_ref, bl_ref, bm_ref, rev):
    C, m = A_CHUNK, A_FBLK
    nch = TM // C
    nsl = C // m - 1
    code = code_ref[...]
    off_mask = code == 1.0
    diag_mask = code == 2.0
    r1 = lax.broadcasted_iota(jnp.int32, (TM, 1), 0) % C
    blk1 = ((C - 1 - r1) if rev else r1) // m
    rows = lambda a, c: a[c * C:(c + 1) * C, :]
    order = range(nch - 1, -1, -1) if rev else range(nch)
    y_parts = []
    for h in range(A_HEADS):
        hs = slice(h * A_HEAD, (h + 1) * A_HEAD)
        qh = q_ref[0, :, hs].astype(F32)
        vb = v_ref[0, :, hs]
        kh = k_ref[0, :, hs].astype(F32)
        bh, blh, bmh = b_ref[:, hs], bl_ref[:, hs], bm_ref[:, hs]
        qt = qh * jnp.exp2(blh)
        q_aug = jnp.concatenate(
            [jnp.where(blk1 == i, qt, 0.0) for i in range(1, nsl + 1)], axis=1).astype(BF16)
        k_parts = []
        for i in range(1, nsl + 1):
            rrow = (C - m * i) if rev else (m * i - 1)
            dif = jnp.concatenate([bh[c * C + rrow:c * C + rrow + 1, :] - rows(bh, c) for c in range(nch)], axis=0)
            k_parts.append(kh * jnp.exp2(jnp.minimum(dif, 0.0)))
        k_aug = jnp.concatenate(k_parts, axis=1).astype(BF16)
        qd = (qh * jnp.exp2(bmh)).astype(BF16)
        kd = (kh * jnp.exp2(-bmh)).astype(BF16)
        s_off = _dot_nt(q_aug, k_aug)
        s_diag = _dot_nt(qd, kd)
        yield
        p = jnp.where(off_mask, s_off, 0.0) + jnp.where(diag_mask, s_diag, 0.0)
        y = _dot(p.astype(BF16), vb)
        yield
        qe = (qh * jnp.exp2(bh)).astype(BF16)
        tots = [bh[c * C:c * C + 1, :] if rev else bh[(c + 1) * C - 1:(c + 1) * C, :] for c in range(nch)]
        upd = [_dot(rows(vb, c).astype(F32).T.astype(BF16),
                    (rows(kh, c) * jnp.exp2(tots[c] - rows(bh, c))).astype(BF16)) for c in range(nch)]
        yield
        st = st_ref[h]
        y_inter = [None] * nch
        for c in order:
            y_inter[c] = _dot_nt(rows(qe, c), st.astype(BF16))
            st = st * jnp.exp2(tots[c]) + upd[c]
        st_ref[h] = st
        y_parts.append(y + jnp.concatenate(y_inter, axis=0))
        yield
    y_ref[0, 0] = jnp.concatenate(y_parts, axis=1).astype(BF16)


def _mixa_prepare(hi_ref, lo_ref, tm_ref, b_ref, bl_ref, bm_ref, result, rev):
    C, m = A_CHUNK, A_FBLK
    worst = jnp.zeros((1, 1), F32)
    lw = 2 * A_HEAD
    for h in range(GROUP_W // lw):
        hs = slice(h * lw, (h + 1) * lw)
        b = _dot(tm_ref[0], hi_ref[0, :, hs]) + _dot(tm_ref[0], lo_ref[0, :, hs])
        starts, mids = [], []
        for r0 in range(0, TM, m):
            pb = (r0 % C) // m
            first = (pb == C // m - 1) if rev else (pb == 0)
            prev_row = (r0 + m) if rev else (r0 - 1)
            mid_row = (r0 + m // 2) if rev else (r0 + m // 2 - 1)
            start = jnp.zeros((1, lw), F32) if first else b[prev_row:prev_row + 1, :]
            starts.append(jnp.broadcast_to(start, (m, lw)))
            mids.append(jnp.broadcast_to(b[mid_row:mid_row + 1, :], (m, lw)))
        bm = b - jnp.concatenate(mids, axis=0)
        b_ref[:, hs] = b
        bl_ref[:, hs] = b - jnp.concatenate(starts, axis=0)
        bm_ref[:, hs] = bm
        worst = jnp.maximum(worst, jnp.max(jnp.abs(bm), keepdims=True))
        yield
    result.append(worst[0, 0] <= A_GUARD * LOG2E)


def _mixb_body(xc_ref, dt_ref, dtb_ref, alog_ref, e_ref, dsk_ref, y_ref, st_ref, tri_ref, rev, d):
    C = B_CHUNK
    hpg = B_HEADS // B_GROUPS
    gw = hpg * B_HEADDIM
    row = lax.broadcasted_iota(jnp.int32, (C, C), 0)
    col = lax.broadcasted_iota(jnp.int32, (C, C), 1)
    trimask = (col >= row) if rev else (col <= row)
    tri = tri_ref[...]
    lane_head = lax.broadcasted_iota(jnp.int32, (1, gw), 1) // B_HEADDIM
    e01 = e_ref[0]
    dtb = dtb_ref[0]
    neg_a = -jnp.exp(alog_ref[0])

    def expand(x):
        hi = x.astype(BF16)
        lo = (x - hi.astype(F32)).astype(BF16)
        return _dot(hi, e01) + _dot(lo, e01)

    for c in (range(TM // C - 1, -1, -1) if rev else range(TM // C)):
        sl = pl.ds(c * C, C)
        bx = xc_ref[0, sl, 0:GROUP_W].astype(F32)
        bb = xc_ref[0, sl, GROUP_W:GROUP_W + B_GROUPS * B_STATE]
        bc = xc_ref[0, sl, GROUP_W + B_GROUPS * B_STATE:XBC_W]
        dt = _softplus(dt_ref[0, sl, :].astype(F32) + dtb)
        la = dt * neg_a
        b = _sel_dot(tri, la)
        b_exp = expand(b)
        dt_exp = _dot(dt.astype(BF16), e01)
        tot_exp = b_exp[0:1, :] if rev else b_exp[C - 1:C, :]
        v_all = bx * dt_exp
        vdec = v_all * jnp.exp(tot_exp - b_exp)
        eb = jnp.exp(b_exp)
        bt = b.T
        yield
        y_parts = []
        for g in range(B_GROUPS):
            gs = slice(g * gw, (g + 1) * gw)
            bg = bb[:, g * B_STATE:(g + 1) * B_STATE]
            cg = bc[:, g * B_STATE:(g + 1) * B_STATE]
            qk = _dot_nt(cg, bg)
            yield
            s_g = st_ref[g]
            vg = v_all[:, gs].astype(BF16)
            m_parts, v_parts = [], []
            for e in range(hpg):
                lane = 8 * d + g * hpg + e
                rel = jnp.where(trimask,
                                jnp.exp(jnp.minimum(b[:, lane:lane + 1] - bt[lane:lane + 1, :], 0.0)), 0.0)
                m_parts.append((qk * rel).astype(BF16))
                v_parts.append(jnp.where(lane_head == e, vg, jnp.zeros_like(vg)))
            y_intra = _dot(jnp.concatenate(m_parts, axis=1), jnp.concatenate(v_parts, axis=0))
            y_state = _dot(cg, s_g.astype(BF16))
            s_upd = _dot(bg.astype(F32).T.astype(BF16), vdec[:, gs].astype(BF16))
            yield
            st_ref[g] = s_g * jnp.exp(tot_exp[:, gs]) + s_upd
            y_parts.append(y_intra + y_state * eb[:, gs])
            yield
        y = jnp.concatenate(y_parts, axis=1)
        if not rev:
            y = y + dsk_ref[...] * bx
        y_ref[0, 0, sl, :] = y.astype(BF16)


def _mixc_body(y_ref, carry_ref, a_ref, b_ref, rev):
    grp = TM // 8
    a3 = a_ref[...].reshape(grp, 8, GROUP_W)
    b3 = b_ref[...].reshape(grp, 8, GROUP_W)
    ro = lax.broadcasted_iota(jnp.int32, (1, 8, 1), 1)
    for s in (1, 2, 4):
        valid = (ro + s <= 7) if rev else (ro >= s)
        shift = (8 - s) if rev else s
        a_sh = pltpu.roll(a3, shift, 1)
        b_sh = pltpu.roll(b3, shift, 1)
        b3 = b3 + a3 * jnp.where(valid, b_sh, 0.0)
        a3 = a3 * jnp.where(valid, a_sh, 1.0)
        yield
    cb = jnp.broadcast_to(carry_ref[...], (8, GROUP_W))
    last = 0 if rev else 7
    hs = [None] * grp
    for g in (range(grp - 1, -1, -1) if rev else range(grp)):
        hg = a3[g] * cb + b3[g]
        hs[g] = hg
        cb = jnp.broadcast_to(hg[last:last + 1, :], (8, GROUP_W))
        if g % 8 == 0:
            yield
    carry_ref[...] = cb[0:1, :]
    y_ref[0, 0] = jnp.concatenate(hs, axis=0).astype(BF16)


def _mixc_gates(x_ref, wa_ref, wi_ref, ba_ref, bi_ref, lam_ref, a_ref, b_ref):
    cxb = x_ref[0]
    cx = cxb.astype(F32)
    tile = 2 * A_HEAD
    bdot = lambda w_ref: jnp.concatenate(
        [_dot(cxb[:, t:t + tile], w_ref[0, t:t + tile, t:t + tile]) for t in range(0, GROUP_W, tile)], axis=1)
    r = _sigmoid(bdot(wa_ref) + ba_ref[0])
    gi = _sigmoid(bdot(wi_ref) + bi_ref[0])
    yield
    la = (-C_POW) * r * _softplus(-lam_ref[0])
    a = jnp.exp(la)
    a_ref[...] = a
    s = 1.0 - a * a
    root = jnp.where(s > 0.0, s * lax.rsqrt(s), 0.0)
    b_ref[...] = root * gi * cx
    yield


def _block_diag(w):
    nd, nb_, n, _ = w.shape
    eye = jnp.eye(nb_, dtype=w.dtype)
    return jnp.einsum("dhij,hg->dhigj", w, eye).reshape(nd, nb_ * n, nb_ * n)


def _mixd_body(q_ref, k_ref, v_ref, lgt_ref, y_ref, st_ref, rev):
    C = D_CHUNK
    lg = -_softplus(-lgt_ref[0])
    row = lax.broadcasted_iota(jnp.int32, (C, C), 0)
    col = lax.broadcasted_iota(jnp.int32, (C, C), 1)
    trimask = (col >= row) if rev else (col <= row)
    dist = jnp.maximum((col - row) if rev else (row - col), 0).astype(F32)
    r1 = lax.broadcasted_iota(jnp.int32, (C, 1), 0)
    tau1 = ((C - 1 - r1) if rev else r1).astype(F32)
    for c in (range(TM // C - 1, -1, -1) if rev else range(TM // C)):
        sl = pl.ds(c * C, C)
        y_parts = []
        for h in range(D_HEADS):
            hs = slice(h * D_HEAD, (h + 1) * D_HEAD)
            lgh = lg[h:h + 1, 0:1]
            qb = q_ref[0, sl, hs]
            kb = k_ref[0, sl, hs]
            vh = v_ref[0, sl, hs]
            kr = kb.astype(F32)
            qk = _dot_nt(qb, kb)
            yield
            rel = jnp.where(trimask, jnp.exp(lgh * dist), 0.0)
            s_h = st_ref[h]
            kdec = kr * jnp.exp(lgh * (C - 1.0 - tau1))
            y_intra = _dot((qk * rel).astype(BF16), vh)
            y_state = _dot(qb, s_h.astype(BF16))
            s_upd = _dot(kdec.T.astype(BF16), vh)
            yield
            st_ref[h] = s_h * jnp.exp(lgh * float(C)) + s_upd
            y_parts.append(y_intra + y_state * jnp.exp(lgh * (tau1 + 1.0)))
            yield
        y_ref[0, 0, sl, :] = jnp.concatenate(y_parts, axis=1).astype(BF16)


def _mix_kernel(aq_ref, ai_ref, ak_ref, ahi_ref, alo_ref,
                bx_ref, bdt_ref, dtb_ref, alog_ref, e_ref, dsk_ref,
                cx_ref, wa_ref, wi_ref, ba_ref, bi_ref, lam_ref,
                dq_ref, dk_ref, dv_ref, lgt_ref,
                ya_ref, yb_ref, yc_ref, yd_ref,
                a_st, a_sh, a_tm, a_code, a_b, a_bl, a_bm, b_st, c_carry, c_a, c_b, d_st, a_bak, b_tri,
                *, nb):
    d = pl.program_id(1)
    j = pl.program_id(2)

    @pl.when(j == 0)
    def _():
        for r in (a_st, a_sh, b_st, c_carry, d_st):
            r[...] = jnp.zeros_like(r)

    def run(rev, dd):
        @pl.when(j == 0)
        def _():
            _mixa_tables(a_tm, a_code, rev)
            row = lax.broadcasted_iota(jnp.int32, (B_CHUNK, B_CHUNK), 0)
            col = lax.broadcasted_iota(jnp.int32, (B_CHUNK, B_CHUNK), 1)
            b_tri[...] = ((col >= row) if rev else (col <= row)).astype(BF16)

        a_bak[...] = a_st[...]
        oks = []
        streams = []
        for s in range(nb):
            one = lambda r, s=s: r.at[pl.ds(s, 1)]
            out = lambda r, s=s: r.at[:, pl.ds(s, 1)]

            def mixer_a(s=s, one=one, out=out):
                yield from _mixa_prepare(one(ahi_ref), one(alo_ref), a_tm, a_b.at[s], a_bl.at[s], a_bm.at[s], oks,
                                         rev)
                yield from _mixa_fast(one(aq_ref), one(ai_ref), out(ya_ref), a_st.at[s], a_code, one(ak_ref),
                                      a_b.at[s], a_bl.at[s], a_bm.at[s], rev)

            def mixer_c(s=s, one=one, out=out):
                yield from _mixc_gates(one(cx_ref), wa_ref, wi_ref, ba_ref, bi_ref, lam_ref, c_a.at[s], c_b.at[s])
                yield from _mixc_body(out(yc_ref), c_carry.at[s], c_a.at[s], c_b.at[s], rev)

            streams += [mixer_a(),
                        _mixb_body(one(bx_ref), one(bdt_ref), dtb_ref, alog_ref, e_ref, dsk_ref, out(yb_ref),
                                   b_st.at[s], b_tri, rev, dd),
                        mixer_c(),
                        _mixd_body(one(dq_ref), one(dk_ref), one(dv_ref), lgt_ref, out(yd_ref), d_st.at[s], rev)]
        while streams:
            for g in list(streams):
                if next(g, _DONE) is _DONE:
                    streams.remove(g)

        for s in range(nb):
            @pl.when(jnp.logical_not(oks[s]))
            def _():
                a_st[s] = a_bak[s]
                _mixa_body(aq_ref.at[pl.ds(s, 1)], ai_ref.at[pl.ds(s, 1)], ak_ref.at[pl.ds(s, 1)],
                           ahi_ref.at[pl.ds(s, 1)], alo_ref.at[pl.ds(s, 1)],
                           ya_ref.at[:, pl.ds(s, 1)], a_st.at[s], a_sh, rev)

    @pl.when(d == 0)
    def _():
        run(False, 0)

    @pl.when(d == 1)
    def _():
        run(True, 1)


def _mix_call(p, dt_bias, a_log, ssm_d, wa, ba, wi, bi, lam, decay_logit, nb):
    b, l, _ = p.shape
    nblk = l // TM
    rowblk = lambda d, j: _seq_block(d, j, nblk)
    pcol = lambda col, width=GROUP_W: pl.BlockSpec((nb, TM, width), lambda i, d, j: (i, rowblk(d, j), col // width))
    dircol = lambda col, step: pl.BlockSpec((nb, TM, GROUP_W),
                                            lambda i, d, j: (i, rowblk(d, j), col // GROUP_W + step * d))
    dirvec = lambda width: pl.BlockSpec((1, 1, width), lambda i, d, j: (d, 0, 0))
    pad_dir = lambda a: jnp.stack([jnp.pad(a[0], (0, DT_W - B_HEADS)),
                                   jnp.pad(a[1], (B_HEADS, DT_W - 2 * B_HEADS))]).reshape(2, 1, DT_W)
    lanes = jnp.arange(GROUP_W) // B_HEADDIM
    e01 = jnp.stack([(jnp.arange(DT_W)[:, None] == lanes[None, :] + 8 * dd) for dd in range(2)]).astype(BF16)
    dsk = jnp.repeat(ssm_d, B_HEADDIM).reshape(1, GROUP_W)
    vec = lambda a: a.reshape(2, 1, GROUP_W)
    wspec = pl.BlockSpec((1, GROUP_W, GROUP_W), lambda i, d, j: (d, 0, 0))
    lgt = jnp.broadcast_to(jnp.pad(decay_logit, ((0, 0), (0, 8 - D_HEADS)))[:, :, None], (2, 8, D_HEAD))
    yspec = pl.BlockSpec((1, nb, TM, GROUP_W), lambda i, d, j: (d, i, rowblk(d, j), 0))
    yshape = jax.ShapeDtypeStruct((2, b, l, GROUP_W), BF16)
    act = lambda: pltpu.VMEM((nb, TM, GROUP_W), F32)
    return pl.pallas_call(
        functools.partial(_mix_kernel, nb=nb),
        grid=(b // nb, 2, nblk),
        in_specs=[pcol(COL_AQ), pcol(COL_AI), dircol(COL_AF, 1), dircol(COL_AHI, 2), dircol(COL_AHI + GROUP_W, 2),
                  pcol(COL_BXBC, XBC_W), pcol(COL_DT, DT_W), dirvec(DT_W), dirvec(DT_W),
                  pl.BlockSpec((1, DT_W, GROUP_W), lambda i, d, j: (d, 0, 0)),
                  pl.BlockSpec((1, GROUP_W), lambda i, d, j: (0, 0)),
                  pcol(COL_CX), wspec, wspec, dirvec(GROUP_W), dirvec(GROUP_W), dirvec(GROUP_W),
                  pcol(COL_DQ), pcol(COL_DK), pcol(COL_DV),
                  pl.BlockSpec((1, 8, D_HEAD), lambda i, d, j: (d, 0, 0))],
        out_specs=[yspec, yspec, yspec, yspec],
        out_shape=[yshape, yshape, yshape, yshape],
        scratch_shapes=[pltpu.VMEM((nb, A_HEADS, A_HEAD, A_HEAD), F32),
                        pltpu.VMEM((3, A_CHUNK + 2 * A_SUB, GROUP_W), F32),
                        pltpu.VMEM((1, TM, TM), BF16),
                        pltpu.VMEM((TM, TM), F32),
                        act(), act(), act(),
                        pltpu.VMEM((nb, B_GROUPS, B_STATE, GROUP_W // B_GROUPS), F32),
                        pltpu.VMEM((nb, 1, GROUP_W), F32), act(), act(),
                        pltpu.VMEM((nb, D_HEADS, D_HEAD, D_HEAD), F32),
                        pltpu.VMEM((nb, A_HEADS, A_HEAD, A_HEAD), F32),
                        pltpu.VMEM((B_CHUNK, B_CHUNK), BF16)],
        compiler_params=_cparams(("parallel", "parallel", "arbitrary")),
        name="mixers",
    )(p, p, p, p, p,
      p, p, pad_dir(dt_bias), pad_dir(a_log), e01, dsk,
      p, _block_diag(wa).astype(BF16), _block_diag(wi).astype(BF16), vec(ba), vec(bi), vec(lam),
      p, p, p, lgt)


def _layer_norm_rows(x, g, b):
    mu = jnp.mean(x, axis=-1, keepdims=True)
    xc = x - mu
    var = jnp.mean(xc * xc, axis=-1, keepdims=True)
    return xc * lax.rsqrt(var + EPS) * g + b


def _outproj_kernel(yaf, yab, ybf, ybb, ycf, ycb, ydf, ydb, ag_ref, bz_ref, cg_ref, dg_ref,
                    h_ref, mod_ref, w_ref, nwa_ref, nwb_ref, lg_ref, lb_ref, o_ref, *, nb):
    rows = nb * TM
    ld = lambda f, bk: (f[0].astype(F32) + bk[0].astype(F32)).reshape(rows, GROUP_W)
    gate = lambda r: r[...].astype(F32).reshape(rows, GROUP_W)
    ya = ld(yaf, yab)
    parts = []
    for h in range(A_HEADS):
        x = ya[:, h * A_HEAD:(h + 1) * A_HEAD]
        parts.append(x * lax.rsqrt(jnp.mean(x * x, axis=-1, keepdims=True) + EPS))
    oa = jnp.concatenate(parts, axis=1) * nwa_ref[...] * _silu(gate(ag_ref))
    yb = ld(ybf, ybb) * _silu(gate(bz_ref))
    gw = GROUP_W // B_GROUPS
    parts = []
    for g in range(B_GROUPS):
        x = yb[:, g * gw:(g + 1) * gw]
        parts.append(x * lax.rsqrt(jnp.mean(x * x, axis=-1, keepdims=True) + EPS))
    ob = jnp.concatenate(parts, axis=1) * nwb_ref[...]
    oc = ld(ycf, ycb) * _gelu_tanh(gate(cg_ref))
    yd = ld(ydf, ydb)
    parts = []
    for h in range(D_HEADS):
        x = yd[:, h * D_HEAD:(h + 1) * D_HEAD]
        mu = jnp.mean(x, axis=-1, keepdims=True)
        xc = x - mu
        parts.append(xc * lax.rsqrt(jnp.mean(xc * xc, axis=-1, keepdims=True) + EPS))
    od = jnp.concatenate(parts, axis=1) * _silu(gate(dg_ref))
    mix = (_dot(oa.astype(BF16), w_ref[0:GROUP_W, :])
           + _dot(ob.astype(BF16), w_ref[GROUP_W:2 * GROUP_W, :])
           + _dot(oc.astype(BF16), w_ref[2 * GROUP_W:3 * GROUP_W, :])
           + _dot(od.astype(BF16), w_ref[3 * GROUP_W:4 * GROUP_W, :]))
    m = mod_ref[:, 0]
    hn = ALPHA * h_ref[...] + m[:, 2:3, :] * mix.reshape(nb, TM, D_MODEL)
    o_ref[...] = _layer_norm_rows(hn, lg_ref[...], lb_ref[...])


def _outproj_call(ya, yb, yc, yd, p, h, mod, w_out_b, nwa, nwb, ln_g, ln_b, nb):
    b, l, d = h.shape
    ydir = lambda dd: pl.BlockSpec((1, nb, TM, GROUP_W), lambda i, t: (dd, i, t, 0))
    pcol = lambda col: pl.BlockSpec((nb, TM, GROUP_W), lambda i, t: (i, t, col // GROUP_W))
    vec = lambda n: pl.BlockSpec((1, n), lambda i, t: (0, 0))
    return pl.pallas_call(
        functools.partial(_outproj_kernel, nb=nb),
        grid=(b // nb, l // TM),
        in_specs=[ydir(0), ydir(1), ydir(0), ydir(1), ydir(0), ydir(1), ydir(0), ydir(1),
                  pcol(COL_AG), pcol(COL_BZ), pcol(COL_CG), pcol(COL_DG),
                  pl.BlockSpec((nb, TM, d), lambda i, t: (i, t, 0)),
                  pl.BlockSpec((nb, 1, 6, d), lambda i, t: (i, jnp.minimum(t, 1), 0, 0)),
                  pl.BlockSpec((4 * GROUP_W, d), lambda i, t: (0, 0), pipeline_mode=pl.Buffered(1)),
                  vec(GROUP_W), vec(GROUP_W), vec(d), vec(d)],
        out_specs=pl.BlockSpec((nb, TM, d), lambda i, t: (i, t, 0)),
        out_shape=jax.ShapeDtypeStruct((b, l, d), F32),
        compiler_params=_cparams(("parallel", "parallel")),
        name="out_proj",
    )(ya, ya, yb, yb, yc, yc, yd, yd, p, p, p, p, h, mod, w_out_b,
      nwa.reshape(1, GROUP_W), nwb.reshape(1, GROUP_W), ln_g.reshape(1, d), ln_b.reshape(1, d))


def _ffn_kernel(h_ref, hp_ref, hn_ref, mod_ref, wg_ref, wu_ref, wo_ref, cw_ref, cb_ref, lg_ref, lb_ref,
                o_ref, act_ref, g_ref, *, nb, nblk, first_blk):
    rb = pl.program_id(1) + first_blk
    prev_ok = rb >= 2
    next_ok = (rb >= 1) & (rb <= nblk - 2)
    m = mod_ref[:, 0]
    sh, sc = m[:, 3:4, :], m[:, 4:5, :]
    hm = h_ref[...]
    hrows = TM + 2 * FFN_HALO
    x_all = jnp.concatenate([hp_ref[...], hm, hn_ref[...]], axis=1) * (1.0 + sc) + sh
    xb_all = x_all.astype(BF16).reshape(nb * hrows, D_MODEL)
    xb = x_all[:, FFN_HALO:FFN_HALO + TM, :].astype(BF16).reshape(nb * TM, D_MODEL)
    r1 = lax.broadcasted_iota(jnp.int32, (1, hrows, 1), 1)
    keep = ((r1 >= FFN_HALO) | prev_ok) & ((r1 < FFN_HALO + TM) | next_ok)
    for c0 in range(0, D_FF, FF_CHUNK):
        cs = slice(c0, c0 + FF_CHUNK)
        g = _dot(xb_all, wg_ref[:, cs]).reshape(nb, hrows, FF_CHUNK)
        up = _dot(xb, wu_ref[:, cs])
        gi = (c0 // FF_CHUNK) % FFN_GBUF
        g_ref[gi] = jnp.where(keep, g, 0.0)
        gc = cb_ref[:, cs].reshape(1, 1, FF_CHUNK)
        for jj in range(3):
            gc = gc + (cw_ref[jj:jj + 1, cs].reshape(1, 1, FF_CHUNK)
                       * g_ref[gi, :, pl.ds(FFN_HALO - 1 + jj, TM), :])
        act_ref[:, cs] =(_gelu_tanh(gc).reshape(nb * TM, FF_CHUNK) * up).astype(BF16)
    f = _dot(act_ref[...], wo_ref[...])
    hn2 = ALPHA * hm + m[:, 5:6, :] * f.reshape(nb, TM, D_MODEL)
    o_ref[...] = _layer_norm_rows(hn2, lg_ref[...], lb_ref[...])


def _ffn_call(h, mod, wg_b, wu_b, wo_b, conv_w, conv_b, ln_g, ln_b, nb, skip_ctx):
    b, l, d = h.shape
    nblk = l // TM
    first = 1 if skip_ctx else 0
    per = TM // FFN_HALO
    last = l // FFN_HALO - 1
    vec = lambda n: pl.BlockSpec((1, n), lambda i, t: (0, 0))
    res = lambda shape: pl.BlockSpec(shape, lambda i, t: (0, 0), pipeline_mode=pl.Buffered(1))
    return pl.pallas_call(
        functools.partial(_ffn_kernel, nb=nb, nblk=nblk, first_blk=first),
        grid=(b // nb, nblk - first),
        in_specs=[pl.BlockSpec((nb, TM, d), lambda i, t: (i, t + first, 0)),
                  pl.BlockSpec((nb, FFN_HALO, d), lambda i, t: (i, jnp.maximum((t + first) * per - 1, 0), 0)),
                  pl.BlockSpec((nb, FFN_HALO, d), lambda i, t: (i, jnp.minimum((t + first + 1) * per, last), 0)),
                  pl.BlockSpec((nb, 1, 6, d), lambda i, t: (i, jnp.minimum(t + first, 1), 0, 0)),
                  res((d, D_FF)), res((d, D_FF)), res((D_FF, d)),
                  pl.BlockSpec((3, D_FF), lambda i, t: (0, 0)), vec(D_FF), vec(d), vec(d)],
        out_specs=pl.BlockSpec((nb, TM, d), lambda i, t: (i, t, 0)),
        out_shape=jax.ShapeDtypeStruct((b, l - first * TM, d), F32),
        scratch_shapes=[pltpu.VMEM((nb * TM, D_FF), BF16),
                        pltpu.VMEM((FFN_GBUF, nb, TM + 2 * FFN_HALO, FF_CHUNK), F32)],
        compiler_params=_cparams(("parallel", "parallel")),
        name="conv_ffn",
    )(h, h, h, mod, wg_b, wu_b, wo_b, conv_w, conv_b.reshape(1, D_FF), ln_g.reshape(1, d), ln_b.reshape(1, d))


def _rope_tables(n):
    rows = n // GRID_W
    row = jnp.repeat(jnp.arange(rows), GRID_W)
    col = jnp.tile(jnp.arange(GRID_W), rows)
    n_freq = D_HEAD // 4
    inv = ROPE_BASE ** (-jnp.arange(n_freq, dtype=F32) / n_freq)
    ang = jnp.concatenate([row[:, None] * inv, col[:, None] * inv], axis=-1)
    ang = jnp.concatenate([jnp.zeros((CTX_LEN, D_HEAD // 2), F32), ang], axis=0)
    cos, sin = jnp.cos(ang), jnp.sin(ang)
    return jnp.concatenate([cos, cos], axis=-1), jnp.concatenate([-sin, sin], axis=-1)


def kernel(x, c, ctx, c_ctx, ada_w, ada_b, w_in, hgrn_lb_logits, hgrn_norm_w, ssm_conv_w, ssm_conv_b, ssm_dt_bias, ssm_a_log, ssm_d, ssm_norm_w, lru_conv_w, lru_conv_b, lru_wa, lru_ba, lru_wi, lru_bi, lru_lambda, ret_decay_logit, w_out, ln1_g, ln1_b, ffn_w_in, ffn_conv_w, ffn_conv_b, ffn_w_out, ln2_g, ln2_b):
    bsz, n, d = x.shape
    assert d == D_MODEL and ctx.shape[1] == CTX_LEN == TM and n % TM == 0 and n % GRID_W == 0
    nb = 2 if bsz % 2 == 0 else 1
    depth = ada_w.shape[0]

    cosf, sinf = _rope_tables(n)
    p_soft = jax.nn.softmax(hgrn_lb_logits.astype(F32), axis=0)
    lb_all = jnp.cumsum(p_soft, axis=0) - p_soft

    rows = -(-(bsz + 1) // 8) * 8
    cond = jnp.zeros((rows, d), F32).at[:bsz].set(c).at[bsz].set(c_ctx)
    mods = _ada_call(cond, ada_w, ada_b)

    n_in = w_in.shape[2]
    dt0 = COL_BXBC + XBC_W
    h = jnp.concatenate([ctx, x], axis=1)
    for l in range(depth):
        ml = mods[l].reshape(rows, 6, d)
        mod = jnp.stack([jnp.broadcast_to(ml[bsz][None], (bsz, 6, d)), ml[:bsz]], axis=1)
        w_l = w_in[l]
        w_perm = jnp.concatenate([w_l[:, :dt0], w_l[:, dt0 + 2 * B_HEADS:], w_l[:, dt0:dt0 + 2 * B_HEADS],
                                  jnp.zeros((d, W_COLS - n_in), F32)], axis=1).astype(BF16)
        conv_w = jnp.concatenate([ssm_conv_w[l], lru_conv_w[l]], axis=1)
        conv_b = jnp.concatenate([ssm_conv_b[l], lru_conv_b[l]]).reshape(1, CONV_HI - CONV_LO)
        p = _inproj_call(h, mod, w_perm, conv_w, conv_b, lb_all[l], cosf, sinf, nb)
        ya, yb, yc, yd = _mix_call(p, ssm_dt_bias[l], ssm_a_log[l], ssm_d[l],
                                   lru_wa[l], lru_ba[l], lru_wi[l], lru_bi[l], lru_lambda[l],
                                   ret_decay_logit[l].astype(F32), nb)
        h = _outproj_call(ya, yb, yc, yd, p, h, mod, w_out[l].astype(BF16), hgrn_norm_w[l], ssm_norm_w[l],
                          ln1_g[l], ln1_b[l], nb)
        wf = ffn_w_in[l]
        h = _ffn_call(h, mod, wf[:, :D_FF].astype(BF16), wf[:, D_FF:].astype(BF16),
                      ffn_w_out[l].astype(BF16), ffn_conv_w[l], ffn_conv_b[l], ln2_g[l], ln2_b[l],
                      nb, skip_ctx=(l == depth - 1))
    return h
```
